```python
import math
import jax
import jax.numpy as jnp
from jax import lax
import numpy as np

D_MODEL = 1024
BATCH = 16
SEQ = 2048
DEPTH = 4

PLE_DIM = 256
EPS = 1e-6
N_BRANCH = 3
GM_GROUPS = 4
GM_GROUP_CH = 128
GM_CHUNK = 128
GM_WIDTH = GM_GROUPS * GM_GROUP_CH
DA_HEADS = 4
DA_HEAD_DIM = 64
DA_V_DIM = 2 * DA_HEAD_DIM
DA_QK_WIDTH = DA_HEADS * 2 * DA_HEAD_DIM
DA_WIDTH = DA_HEADS * DA_V_DIM
ROPE_THETA = 500000.0
ROPE_DIM = DA_HEAD_DIM // 4
Q_BLOCK = 128
DN_HEADS = 4
DN_HEAD_DIM = 128
DN_WIDTH = DN_HEADS * DN_HEAD_DIM
DN_CONV = 4
DN_CHUNK = 64
SPLIT_SIZES = (GM_WIDTH, GM_WIDTH, GM_WIDTH, DA_QK_WIDTH, DA_QK_WIDTH, DA_WIDTH, DA_WIDTH, 3 * DN_WIDTH, DN_HEADS, DN_HEADS, DN_WIDTH, N_BRANCH * D_MODEL)
N_IN = sum(SPLIT_SIZES)

kernel_name = 'hybrid_gmlp_diffattn_gdn_block'


def rms_norm(x, g):
    xf = x.astype(jnp.float32)
    y = xf * lax.rsqrt(jnp.mean(xf * xf, axis=-1, keepdims=True) + EPS)
    return (y * g.astype(jnp.float32)).astype(x.dtype)


def layer_norm(x, g, b):
    xf = x.astype(jnp.float32)
    mu = jnp.mean(xf, axis=-1, keepdims=True)
    xc = xf - mu
    y = xc * lax.rsqrt(jnp.mean(xc * xc, axis=-1, keepdims=True) + EPS)
    return (y * g.astype(jnp.float32) + b.astype(jnp.float32)).astype(x.dtype)


def l2_norm(x):
    return x * lax.rsqrt(jnp.sum(x * x, axis=-1, keepdims=True) + EPS)


def partial_rotary(x, cos, sin):
    half = ROPE_DIM // 2
    xf = x.astype(jnp.float32)
    x1 = xf[..., :half]
    x2 = xf[..., half:ROPE_DIM]
    out = jnp.concatenate([x1 * cos - x2 * sin, x2 * cos + x1 * sin, xf[..., ROPE_DIM:]], axis=-1)
    return out.astype(x.dtype)


def causal_dwconv(x, w):
    k = w.shape[0]
    return lax.conv_general_dilated(x, w[:, None, :].astype(x.dtype), window_strides=(1,), padding=[(k - 1, 0)], dimension_numbers=('NWC', 'WIO', 'NWC'), feature_group_count=x.shape[-1])


def chunked_gmlp(u, v, ln_g, ln_b, w_s, b_s):
    bsz, s, _ = v.shape
    n = s // GM_CHUNK
    u = jax.nn.gelu(u)
    v = layer_norm(jax.nn.gelu(v), ln_g, ln_b)
    tri = jnp.tril(jnp.ones((GM_CHUNK, GM_CHUNK), w_s.dtype))
    vr = v.reshape(bsz, n, GM_CHUNK, GM_GROUPS, GM_GROUP_CH)
    mix = jnp.einsum('gts,bnsgc->bntgc', w_s * tri, vr) + b_s.T[:, :, None]
    return u * mix.reshape(bsz, s, GM_WIDTH)


def diff_attention(q, k, v, positions, lam_q1, lam_k1, lam_q2, lam_k2, subln_g, lambda_init):
    bsz, s, _ = q.shape
    q = q.reshape(bsz, s, DA_HEADS, 2, DA_HEAD_DIM).transpose(0, 2, 3, 1, 4)
    k = k.reshape(bsz, s, DA_HEADS, 2, DA_HEAD_DIM).transpose(0, 2, 3, 1, 4)
    v = v.reshape(bsz, s, DA_HEADS, DA_V_DIM).transpose(0, 2, 1, 3)
    inv_freq = ROPE_THETA ** (-jnp.arange(0, ROPE_DIM, 2, dtype=jnp.float32) / ROPE_DIM)
    ang = positions.astype(jnp.float32)[..., None] * inv_freq
    cos = jnp.cos(ang)[:, None, None]
    sin = jnp.sin(ang)[:, None, None]
    q = partial_rotary(q, cos, sin)
    k = partial_rotary(k, cos, sin)
    f32 = jnp.float32
    lam = jnp.exp(jnp.sum(lam_q1.astype(f32) * lam_k1.astype(f32))) - jnp.exp(jnp.sum(lam_q2.astype(f32) * lam_k2.astype(f32))) + lambda_init
    scale = DA_HEAD_DIM ** -0.5
    outs = []
    for blk in range(s // Q_BLOCK):
        lo, hi = blk * Q_BLOCK, (blk + 1) * Q_BLOCK
        sc = jnp.einsum('bhcqd,bhckd->bhcqk', q[:, :, :, lo:hi], k[:, :, :, :hi]).astype(f32) * scale
        mask = (lo + jnp.arange(Q_BLOCK))[:, None] >= jnp.arange(hi)[None, :]
        a = jax.nn.softmax(jnp.where(mask, sc, -jnp.inf), axis=-1)
        w = a[:, :, 0] - lam * a[:, :, 1]
        outs.append(jnp.einsum('bhqk,bhkd->bhqd', w.astype(v.dtype), v[:, :, :hi]))
    o = jnp.concatenate(outs, axis=2)
    o = rms_norm(o, subln_g) * (1.0 - lambda_init)
    return o.transpose(0, 2, 1, 3).reshape(bsz, s, DA_WIDTH)


def chunk_gated_delta_rule(q, k, v, g, beta):
    bsz, h, s, dk = q.shape
    dv = v.shape[-1]
    n, c = s // DN_CHUNK, DN_CHUNK
    rs = lambda t: t.reshape(bsz, h, n, c, *t.shape[3:])
    q, k, v, g, beta = rs(q), rs(k), rs(v), rs(g), rs(beta)
    g = jnp.cumsum(g, axis=-1)
    idx = jnp.arange(c)
    causal = idx[:, None] >= idx[None, :]
    strict = idx[:, None] > idx[None, :]
    decay = jnp.exp(jnp.where(causal, g[..., :, None] - g[..., None, :], -jnp.inf))
    kk = jnp.einsum('bhncd,bhnsd->bhncs', k, k)
    a_strict = jnp.where(strict, beta[..., :, None] * kk * decay, 0.0)
    rhs = jnp.concatenate([v * beta[..., None], k * (beta * jnp.exp(g))[..., None]], axis=-1)
    sol = lax.linalg.triangular_solve(a_strict, rhs, left_side=True, lower=True, unit_diagonal=True)
    u, w = sol[..., :dv], sol[..., dv:]
    qk = jnp.einsum('bhncd,bhnsd->bhncs', q, k) * decay
    g_last = g[..., -1]
    q_dec = q * jnp.exp(g)[..., None]
    k_dec = k * jnp.exp(g_last[..., None] - g)[..., None]

    def step(state, xs):
        u_i, w_i, qk_i, qd_i, kd_i, gl_i = xs
        v_new = u_i - jnp.einsum('bhck,bhkv->bhcv', w_i, state)
        o_i = jnp.einsum('bhck,bhkv->bhcv', qd_i, state) + jnp.einsum('bhcs,bhsv->bhcv', qk_i, v_new)
        state = state * jnp.exp(gl_i)[..., None, None] + jnp.einsum('bhck,bhcv->bhkv', kd_i, v_new)
        return state, o_i

    xs = tuple(jnp.moveaxis(t, 2, 0) for t in (u, w, qk, q_dec, k_dec, g_last))
    state0 = jnp.zeros((bsz, h, dk, dv), jnp.float32)
    _, o = lax.scan(step, state0, xs)
    return jnp.moveaxis(o, 0, 2).reshape(bsz, h, s, dv)


def gated_deltanet(qkv, a, b, conv_w, a_log, dt_bias, norm_g):
    bsz, s, _ = qkv.shape
    f32 = jnp.float32
    qkv = jax.nn.silu(causal_dwconv(qkv, conv_w)).astype(f32)
    q, k, v = jnp.split(qkv, 3, axis=-1)
    heads = lambda t: t.reshape(bsz, s, DN_HEADS, DN_HEAD_DIM).transpose(0, 2, 1, 3)
    q = l2_norm(heads(q)) * (DN_HEAD_DIM ** -0.5)
    k = l2_norm(heads(k))
    v = heads(v)
    g = -jnp.exp(a_log.astype(f32)) * jax.nn.softplus(a.astype(f32) + dt_bias.astype(f32))
    beta = jax.nn.sigmoid(b.astype(f32))
    o = chunk_gated_delta_rule(q, k, v, g.transpose(0, 2, 1), beta.transpose(0, 2, 1))
    o = rms_norm(o, norm_g)
    return o.transpose(0, 2, 1, 3).reshape(bsz, s, DN_WIDTH)


def setup_inputs(seed: int = 0) -> dict:
    key = jax.random.key(seed)
    ks = jax.random.split(key, 32)
    f32 = jnp.float32

    def nrm(k, shape, scale):
        return jax.random.normal(k, shape, f32) * scale

    def gain(k, shape):
        return 1.0 + 0.02 * jax.random.normal(k, shape, f32)

    x = nrm(ks[0], (BATCH, SEQ, D_MODEL), 1.0)
    p = nrm(ks[1], (DEPTH, BATCH, SEQ, PLE_DIM), 1.0)
    offsets = jax.random.randint(ks[2], (BATCH, 1), 0, 4096, dtype=jnp.int32)
    positions = offsets + jnp.arange(SEQ, dtype=jnp.int32)[None, :]
    return {
        'x': x,
        'p': p,
        'positions': positions,
        'norm_g': gain(ks[3], (DEPTH, D_MODEL)),
        'w_in': nrm(ks[4], (DEPTH, D_MODEL, N_IN), D_MODEL ** -0.5),
        'gm_ln_g': gain(ks[5], (DEPTH, GM_WIDTH)),
        'gm_ln_b': nrm(ks[6], (DEPTH, GM_WIDTH), 0.02),
        'gm_ws': nrm(ks[7], (DEPTH, GM_GROUPS, GM_CHUNK, GM_CHUNK), GM_CHUNK ** -0.5),
        'gm_bs': 1.0 + nrm(ks[8], (DEPTH, GM_GROUPS, GM_CHUNK), 0.1),
        'da_lq1': nrm(ks[9], (DEPTH, DA_HEAD_DIM), 0.1),
        'da_lk1': nrm(ks[10], (DEPTH, DA_HEAD_DIM), 0.1),
        'da_lq2': nrm(ks[11], (DEPTH, DA_HEAD_DIM), 0.1),
        'da_lk2': nrm(ks[12], (DEPTH, DA_HEAD_DIM), 0.1),
        'da_subln_g': gain(ks[13], (DEPTH, DA_V_DIM)),
        'dn_conv_w': nrm(ks[14], (DEPTH, DN_CONV, 3 * DN_WIDTH), DN_CONV ** -0.5),
        'dn_a_log': jnp.log(jax.random.uniform(ks[15], (DEPTH, DN_HEADS), f32, 1.0, 16.0)),
        'dn_dt_bias': nrm(ks[16], (DEPTH, DN_HEADS), 0.1),
        'dn_norm_g': gain(ks[17], (DEPTH, DN_HEAD_DIM)),
        'w_br_a': nrm(ks[18], (DEPTH, GM_WIDTH, D_MODEL), GM_WIDTH ** -0.5),
        'w_br_b': nrm(ks[19], (DEPTH, DA_WIDTH, D_MODEL), DA_WIDTH ** -0.5),
        'w_br_c': nrm(ks[20], (DEPTH, DN_WIDTH, D_MODEL), DN_WIDTH ** -0.5),
        'w_out': nrm(ks[21], (DEPTH, D_MODEL, D_MODEL), D_MODEL ** -0.5),
        'ple_norm_g': gain(ks[22], (DEPTH, D_MODEL)),
        'w_ple_gate': nrm(ks[23], (DEPTH, D_MODEL, D_MODEL), D_MODEL ** -0.5),
        'w_ple_proj': nrm(ks[24], (DEPTH, PLE_DIM, D_MODEL), PLE_DIM ** -0.5),
        'final_norm_g': gain(ks[25], (D_MODEL,)),
    }


def reference(x, p, positions, norm_g, w_in, gm_ln_g, gm_ln_b, gm_ws, gm_bs, da_lq1, da_lk1, da_lq2, da_lk2, da_subln_g, dn_conv_w, dn_a_log, dn_dt_bias, dn_norm_g, w_br_a, w_br_b, w_br_c, w_out, ple_norm_g, w_ple_gate, w_ple_proj, final_norm_g):
    split_idx = np.cumsum(SPLIT_SIZES)[:-1].tolist()
    for i in range(DEPTH):
        lambda_init = 0.8 - 0.6 * math.exp(-0.3 * i)
        h = rms_norm(x, norm_g[i])
        proj = h @ w_in[i]
        (gm_u, gm_v, gm_z, da_q, da_k, da_v, da_z, dn_qkv, dn_a, dn_b, dn_z, gates) = jnp.split(proj, split_idx, axis=-1)
        y_a = chunked_gmlp(gm_u, gm_v, gm_ln_g[i], gm_ln_b[i], gm_ws[i], gm_bs[i]) * jax.nn.silu(gm_z)
        y_b = diff_attention(da_q, da_k, da_v, positions, da_lq1[i], da_lk1[i], da_lq2[i], da_lk2[i], da_subln_g[i], lambda_init) * jax.nn.silu(da_z)
        y_c = gated_deltanet(dn_qkv, dn_a, dn_b, dn_conv_w[i], dn_a_log[i], dn_dt_bias[i], dn_norm_g[i]).astype(x.dtype) * jax.nn.silu(dn_z)
        g_a, g_b, g_c = jnp.split(jax.nn.sigmoid(gates), N_BRANCH, axis=-1)
        merged = g_a * (y_a @ w_br_a[i]) + g_b * (y_b @ w_br_b[i]) + g_c * (y_c @ w_br_c[i])
        x = x + merged @ w_out[i]
        ple_gate = jax.nn.sigmoid(rms_norm(x, ple_norm_g[i]) @ w_ple_gate[i])
        x = x + ple_gate * (p[i] @ w_ple_proj[i])
    return rms_norm(x, final_norm_g)
```

```python
import functools
import math

import jax
import jax.numpy as jnp
import numpy as np
from jax import lax
from jax.experimental import pallas as pl
from jax.experimental.pallas import tpu as pltpu

F32 = jnp.float32
BF16 = jnp.bfloat16

D_MODEL = 1024
PLE_DIM = 256
EPS = 1e-6
GM_GROUPS = 4
GM_CHUNK = 128
GM_WIDTH = 512
DA_HEADS = 4
DA_HEAD_DIM = 64
ROPE_THETA = 500000.0
ROPE_DIM = 16
DN_HEADS = 4
DN_HEAD_DIM = 128
DN_WIDTH = 512
DN_CONV = 4

LANES = 128
COL_BLOCK = 512
N_COL_BLOCKS = 17
CB_GATES = 0
CB_GM_U, CB_GM_V, CB_GM_Z = 6, 7, 8
CB_DN_QKV = 9
CB_DA_Q, CB_DA_K, CB_DA_V, CB_DA_Z = 12, 13, 14, 15
CB_DN_Z = 16
VMEM_LIMIT = 56 * 1024 * 1024
DN_CHUNK_ROWS = 256
CONV_PAD = 8


def _sigmoid(x):
    return 1.0 / (1.0 + jnp.exp(-x))


def _gelu_tanh(x):
    c = math.sqrt(2.0 / math.pi)
    return 0.5 * x * (1.0 + jnp.tanh(c * (x + 0.044715 * (x * x * x))))


def _nt_dot(a, b):
    return lax.dot_general(a, b, (((1,), (1,)), ((), ())), preferred_element_type=F32)


def _inproj_kernel(x_ref, ng_ref, w_ref, wab_ref, lng_ref, lnb_ref, tc_ref, ts_ref,
                   alog_ref, dtb_ref, proj_ref, ab_ref, h_ref, *, rc):
    j = pl.program_id(1)
    tm = x_ref.shape[0]
    nchunk = tm // rc

    @pl.when(j == 0)
    def _():
        def body(c, carry):
            r = pl.ds(pl.multiple_of(c * rc, rc), rc)
            xv = x_ref[r, :]
            ms = jnp.mean(xv * xv, axis=-1, keepdims=True)
            hb = (xv * lax.rsqrt(ms + EPS) * ng_ref[...]).astype(BF16)
            h_ref[r, :] = hb
            z = jnp.dot(hb, wab_ref[...], preferred_element_type=F32)
            lane = lax.broadcasted_iota(jnp.int32, z.shape, 1)
            zz = z + dtb_ref[...]
            sp = jnp.maximum(zz, 0.0) + jnp.log1p(jnp.exp(-jnp.abs(zz)))
            gval = -jnp.exp(alog_ref[...]) * sp
            ab_ref[r, :] = jnp.where(lane < DN_HEADS, gval, _sigmoid(z))
            return carry
        lax.fori_loop(0, nchunk, body, 0)

    is_sig = j < CB_GM_U
    is_gelu = j == CB_GM_U
    is_geln = j == CB_GM_V
    is_silu = (j == CB_GM_Z) | (j == CB_DA_Z) | (j == CB_DN_Z)
    is_rot = (j == CB_DA_Q) | (j == CB_DA_K)
    is_id = ((j >= CB_DN_QKV) & (j < CB_DA_Q)) | (j == CB_DA_V)

    def body(c, carry):
        r = pl.ds(pl.multiple_of(c * rc, rc), rc)
        acc = jnp.dot(h_ref[r, :], w_ref[...], preferred_element_type=F32)

        @pl.when(is_sig)
        def _():
            proj_ref[r, :] = _sigmoid(acc).astype(BF16)

        @pl.when(is_gelu)
        def _():
            proj_ref[r, :] = _gelu_tanh(acc).astype(BF16)

        @pl.when(is_geln)
        def _():
            g = _gelu_tanh(acc)
            mu = jnp.mean(g, axis=-1, keepdims=True)
            gc = g - mu
            var = jnp.mean(gc * gc, axis=-1, keepdims=True)
            y = gc * lax.rsqrt(var + EPS) * lng_ref[...] + lnb_ref[...]
            proj_ref[r, :] = y.astype(BF16)

        @pl.when(is_silu)
        def _():
            proj_ref[r, :] = (acc * _sigmoid(acc)).astype(BF16)

        @pl.when(is_id)
        def _():
            proj_ref[r, :] = acc.astype(BF16)

        @pl.when(is_rot)
        def _():
            scale = jnp.where(j == CB_DA_Q, DA_HEAD_DIM ** -0.5, 1.0).astype(F32)
            tcv = tc_ref[r, :]
            tsv = ts_ref[r, :]
            lane = lax.broadcasted_iota(jnp.int32, (rc, LANES), 1)
            first_half = (lane & (DA_HEAD_DIM - 1)) < (ROPE_DIM // 2)
            for s in range(COL_BLOCK // LANES):
                a = acc[:, s * LANES:(s + 1) * LANES]
                up = pltpu.roll(a, LANES - ROPE_DIM // 2, 1)
                dn = pltpu.roll(a, ROPE_DIM // 2, 1)
                o = a * tcv + jnp.where(first_half, up, dn) * tsv
                proj_ref[r, s * LANES:(s + 1) * LANES] = (o * scale).astype(BF16)

        return carry

    lax.fori_loop(0, nchunk, body, 0)


def _inproj(x2, ng, w, wab, lng, lnb, tabc, tabs, alog, dtb, *, seq):
    t = x2.shape[0]
    tm = seq
    rc = 256
    grid = (t // tm, N_COL_BLOCKS)
    const = lambda i, j: (0, 0)
    rows = lambda i, j: (i, 0)
    return pl.pallas_call(
        functools.partial(_inproj_kernel, rc=rc),
        grid=grid,
        in_specs=[
            pl.BlockSpec((tm, D_MODEL), rows),
            pl.BlockSpec((1, D_MODEL), const),
            pl.BlockSpec((D_MODEL, COL_BLOCK), lambda i, j: (0, j)),
            pl.BlockSpec((D_MODEL, LANES), const),
            pl.BlockSpec((1, COL_BLOCK), const),
            pl.BlockSpec((1, COL_BLOCK), const),
            pl.BlockSpec((tm, LANES), rows),
            pl.BlockSpec((tm, LANES), rows),
            pl.BlockSpec((1, LANES), const),
            pl.BlockSpec((1, LANES), const),
        ],
        out_specs=[
            pl.BlockSpec((tm, COL_BLOCK), lambda i, j: (i, j)),
            pl.BlockSpec((tm, LANES), rows),
        ],
        out_shape=[
            jax.ShapeDtypeStruct((t, N_COL_BLOCKS * COL_BLOCK), BF16),
            jax.ShapeDtypeStruct((t, LANES), F32),
        ],
        scratch_shapes=[pltpu.VMEM((tm, D_MODEL), BF16)],
        compiler_params=pltpu.CompilerParams(
            dimension_semantics=("arbitrary", "arbitrary"), vmem_limit_bytes=VMEM_LIMIT),
        name="inproj",
    )(x2, ng, w, wab, lng, lnb, tabc, tabs, alog, dtb)


def _attn_kernel(q_ref, k_ref, v_ref, z_ref, lam_ref, sg_ref, o_ref, *, lambda_init, tk):
    qi = pl.program_id(2)
    tq = q_ref.shape[0]
    q = q_ref[...]
    lane = lax.broadcasted_iota(jnp.int32, q.shape, 1)
    zero = jnp.zeros_like(q)
    qs = (jnp.where(lane < DA_HEAD_DIM, q, zero), jnp.where(lane >= DA_HEAD_DIM, q, zero))

    lp = lam_ref[...]
    lam = (jnp.exp(jnp.sum(lp[0:1] * lp[1:2], axis=-1, keepdims=True))
           - jnp.exp(jnp.sum(lp[2:3] * lp[3:4], axis=-1, keepdims=True)) + lambda_init)

    def block(kb, carry, masked):
        kr = pl.ds(pl.multiple_of(kb * tk, tk), tk)
        ks = k_ref[kr, :]
        vs = v_ref[kr, :]
        out = []
        for c in range(2):
            m, l, a = carry[c]
            s = _nt_dot(qs[c], ks)
            if masked:
                ri = lax.broadcasted_iota(jnp.int32, s.shape, 0)
                ci = lax.broadcasted_iota(jnp.int32, s.shape, 1)
                s = jnp.where(ri >= ci, s, -jnp.inf)
            mn = jnp.maximum(m, jnp.max(s, axis=-1, keepdims=True))
            alpha = jnp.exp(m - mn)
            p = jnp.exp(s - mn)
            l = alpha * l + jnp.sum(p, axis=-1, keepdims=True)
            a = alpha * a + jnp.dot(p.astype(BF16), vs, preferred_element_type=F32)
            out.append((mn, l, a))
        return tuple(out)

    init = tuple((jnp.full((tq, 1), -1e30, F32), jnp.zeros((tq, 1), F32),
                  jnp.zeros((tq, LANES), F32)) for _ in range(2))
    carry = lax.fori_loop(0, qi, lambda kb, cr: block(kb, cr, False), init)
    (_, l0, a0), (_, l1, a1) = block(qi, carry, True)

    o = a0 / l0 - lam * (a1 / l1)
    o = o * lax.rsqrt(jnp.mean(o * o, axis=-1, keepdims=True) + EPS) * sg_ref[...]
    o = o * (1.0 - lambda_init)
    o_ref[...] = (o * z_ref[...].astype(F32)).astype(BF16)


def _attn(proj, lam_p, subln_g, *, batch, seq, lambda_init):
    t = proj.shape[0]
    tq = 256
    nq = seq // tq
    per = COL_BLOCK // LANES
    grid = (batch, DA_HEADS, nq)
    return pl.pallas_call(
        functools.partial(_attn_kernel, lambda_init=lambda_init, tk=tq),
        grid=grid,
        in_specs=[
            pl.BlockSpec((tq, LANES), lambda b, h, i: (b * nq + i, CB_DA_Q * per + h)),
            pl.BlockSpec((seq, LANES), lambda b, h, i: (b, CB_DA_K * per + h)),
            pl.BlockSpec((seq, LANES), lambda b, h, i: (b, CB_DA_V * per + h)),
            pl.BlockSpec((tq, LANES), lambda b, h, i: (b * nq + i, CB_DA_Z * per + h)),
            pl.BlockSpec((8, LANES), lambda b, h, i: (0, 0)),
            pl.BlockSpec((1, LANES), lambda b, h, i: (0, 0)),
        ],
        out_specs=pl.BlockSpec((tq, LANES), lambda b, h, i: (b * nq + i, h)),
        out_shape=jax.ShapeDtypeStruct((t, DA_HEADS * LANES), BF16),
        compiler_params=pltpu.CompilerParams(
            dimension_semantics=("arbitrary", "arbitrary", "arbitrary"), vmem_limit_bytes=VMEM_LIMIT),
        name="attn",
    )(proj, proj, proj, proj, lam_p, subln_g)


def _segment_cumsum(x, axis, seg):
    idx = lax.broadcasted_iota(jnp.int32, x.shape, axis) & (seg - 1)
    s = 1
    while s < seg:
        x = x + jnp.where(idx >= s, pltpu.roll(x, s, axis), 0.0)
        s *= 2
    return x


def _gdn_kernel(qkv_ref, z_ref, ab_ref, abt_ref, cw_ref, ng_ref, o_ref, halo_ref, st_ref):
    seq = qkv_ref.shape[0]
    cr = DN_CHUNK_ROWS
    ngroups = seq // cr
    nh = DN_HEADS

    halo_ref[...] = jnp.zeros(halo_ref.shape, F32)
    st_ref[...] = jnp.zeros(st_ref.shape, F32)

    def conv_silu(r, part, hh):
        col = part * DN_WIDTH + hh * LANES
        idx = part * nh + hh
        cur = qkv_ref[r, col:col + LANES].astype(F32)
        xa = jnp.concatenate([halo_ref[idx], cur], axis=0)
        halo_ref[idx] = cur[cr - CONV_PAD:, :]
        w = cw_ref[:, col:col + LANES]
        y = cur * w[DN_CONV - 1:DN_CONV, :]
        for tap in range(DN_CONV - 1):
            lo = CONV_PAD - (DN_CONV - 1 - tap)
            y = y + xa[lo:lo + cr, :] * w[tap:tap + 1, :]
        return y * _sigmoid(y)

    def l2n(y):
        return y * lax.rsqrt(jnp.sum(y * y, axis=-1, keepdims=True) + EPS)

    def group(g, carry):
        r = pl.ds(pl.multiple_of(g * cr, cr), cr)
        abv = ab_ref[r, :]
        gcol = _segment_cumsum(abv, 0, cr)
        grow = _segment_cumsum(abt_ref[:, r], 1, cr)
        ri = lax.broadcasted_iota(jnp.int32, (cr, cr), 0)
        ci = lax.broadcasted_iota(jnp.int32, (cr, cr), 1)
        causal = ri >= ci
        strict = ri > ci
        for hh in range(nh):
            gcc = gcol[:, hh:hh + 1]
            gcr = grow[hh:hh + 1, :]
            bcol = abv[:, nh + hh:nh + hh + 1]
            decay = jnp.exp(jnp.where(causal, gcc - gcr, -jnp.inf))
            qf = l2n(conv_silu(r, 0, hh)) * (DN_HEAD_DIM ** -0.5)
            kf = l2n(conv_silu(r, 1, hh))
            vf = conv_silu(r, 2, hh)
            k16 = kf.astype(BF16)
            a = jnp.where(strict, (bcol * _nt_dot(k16, k16)) * decay, 0.0)
            n = -a
            pw = a
            for _ in range(int(math.log2(cr)) - 1):
                pw16 = pw.astype(BF16)
                pw = jnp.dot(pw16, pw16, preferred_element_type=F32)
                n = n + pw + jnp.dot(n.astype(BF16), pw.astype(BF16), preferred_element_type=F32)
            eg = jnp.exp(gcc)
            rhs = jnp.concatenate([vf * bcol, kf * (bcol * eg)], axis=1)
            uw = rhs + jnp.dot(n.astype(BF16), rhs.astype(BF16), preferred_element_type=F32)
            u = uw[:, :LANES]
            w = uw[:, LANES:]
            qk = _nt_dot(qf.astype(BF16), k16) * decay
            gl = gcc[cr - 1:cr, :]
            k_dec = kf * jnp.exp(gl - gcc)
            st = st_ref[hh]
            st16 = st.astype(BF16)
            v_new = u - jnp.dot(w.astype(BF16), st16, preferred_element_type=F32)
            v16 = v_new.astype(BF16)
            o = (jnp.dot((qf * eg).astype(BF16), st16, preferred_element_type=F32)
                 + jnp.dot(qk.astype(BF16), v16, preferred_element_type=F32))
            st_ref[hh] = st * jnp.exp(gl) + jnp.dot(k_dec.T.astype(BF16), v16, preferred_element_type=F32)
            o = o * lax.rsqrt(jnp.mean(o * o, axis=-1, keepdims=True) + EPS) * ng_ref[...]
            zc = z_ref[r, hh * LANES:(hh + 1) * LANES].astype(F32)
            o_ref[r, hh * LANES:(hh + 1) * LANES] = (o * zc).astype(BF16)
        return carry

    lax.fori_loop(0, ngroups, group, 0)


def _gdn(proj, ab, abt, conv_w, norm_g, *, batch, seq):
    t = proj.shape[0]
    return pl.pallas_call(
        _gdn_kernel,
        grid=(batch,),
        in_specs=[
            pl.BlockSpec((seq, 3 * DN_WIDTH), lambda b: (b, CB_DN_QKV // 3)),
            pl.BlockSpec((seq, DN_WIDTH), lambda b: (b, CB_DN_Z)),
            pl.BlockSpec((seq, LANES), lambda b: (b, 0)),
            pl.BlockSpec((None, 8, seq), lambda b: (b, 0, 0)),
            pl.BlockSpec((DN_CONV, 3 * DN_WIDTH), lambda b: (0, 0)),
            pl.BlockSpec((1, LANES), lambda b: (0, 0)),
        ],
        out_specs=pl.BlockSpec((seq, DN_WIDTH), lambda b: (b, 0)),
        out_shape=jax.ShapeDtypeStruct((t, DN_WIDTH), BF16),
        scratch_shapes=[
            pltpu.VMEM((3 * DN_HEADS, CONV_PAD, LANES), F32),
            pltpu.VMEM((DN_HEADS, DN_HEAD_DIM, DN_HEAD_DIM), F32),
        ],
        compiler_params=pltpu.CompilerParams(
            dimension_semantics=("arbitrary",), vmem_limit_bytes=VMEM_LIMIT),
        name="gdn",
    )(proj, proj, ab, abt, conv_w, norm_g)


def _merge_kernel(x_ref, gates_ref, gm_ref, yb_ref, yc_ref, p_ref, ws_ref, bst_ref,
                  wa_ref, wb_ref, wc_ref, wo_ref, png_ref, wpg_ref, wpp_ref, fng_ref,
                  o_ref, ya_ref, *, rc, final):
    tm = x_ref.shape[0]

    ri = lax.broadcasted_iota(jnp.int32, (GM_CHUNK, GM_CHUNK), 0)
    ci = lax.broadcasted_iota(jnp.int32, (GM_CHUNK, GM_CHUNK), 1)
    wtri = [jnp.where(ri >= ci, ws_ref[g], 0.0).astype(BF16) for g in range(GM_GROUPS)]

    def gm_body(c, carry):
        r = pl.ds(pl.multiple_of(c * GM_CHUNK, GM_CHUNK), GM_CHUNK)
        for g in range(GM_GROUPS):
            cols = slice(g * LANES, (g + 1) * LANES)
            u = gm_ref[r, cols].astype(F32)
            v = gm_ref[r, GM_WIDTH + g * LANES:GM_WIDTH + (g + 1) * LANES]
            z = gm_ref[r, 2 * GM_WIDTH + g * LANES:2 * GM_WIDTH + (g + 1) * LANES].astype(F32)
            mix = jnp.dot(wtri[g], v, preferred_element_type=F32) + bst_ref[:, g:g + 1]
            ya_ref[r, cols] = (u * mix * z).astype(BF16)
        return carry
    lax.fori_loop(0, tm // GM_CHUNK, gm_body, 0)

    def body(c, carry):
        r = pl.ds(pl.multiple_of(c * rc, rc), rc)
        ga = gates_ref[r, 0:D_MODEL].astype(F32)
        gb = gates_ref[r, D_MODEL:2 * D_MODEL].astype(F32)
        gc = gates_ref[r, 2 * D_MODEL:3 * D_MODEL].astype(F32)
        m = (ga * jnp.dot(ya_ref[r, :], wa_ref[...], preferred_element_type=F32)
             + gb * jnp.dot(yb_ref[r, :], wb_ref[...], preferred_element_type=F32)
             + gc * jnp.dot(yc_ref[r, :], wc_ref[...], preferred_element_type=F32))
        x1 = x_ref[r, :] + jnp.dot(m.astype(BF16), wo_ref[...], preferred_element_type=F32)
        hn = x1 * lax.rsqrt(jnp.mean(x1 * x1, axis=-1, keepdims=True) + EPS) * png_ref[...]
        gate = _sigmoid(jnp.dot(hn.astype(BF16), wpg_ref[...], preferred_element_type=F32))
        pp = jnp.dot(p_ref[r, :].astype(BF16), wpp_ref[...], preferred_element_type=F32)
        x2 = x1 + gate * pp
        if final:
            x2 = x2 * lax.rsqrt(jnp.mean(x2 * x2, axis=-1, keepdims=True) + EPS) * fng_ref[...]
        o_ref[r, :] = x2
        return carry
    lax.fori_loop(0, tm // rc, body, 0)


def _merge(x2, proj, yb, yc, p2, ws, bst, wa, wb, wc, wo, png, wpg, wpp, fng, *, final):
    t = x2.shape[0]
    tm = 512
    rc = 256
    rows = lambda i: (i, 0)
    const2 = lambda i: (0, 0)
    return pl.pallas_call(
        functools.partial(_merge_kernel, rc=rc, final=final),
        grid=(t // tm,),
        in_specs=[
            pl.BlockSpec((tm, D_MODEL), rows),
            pl.BlockSpec((tm, 3 * D_MODEL), lambda i: (i, CB_GATES)),
            pl.BlockSpec((tm, 3 * GM_WIDTH), lambda i: (i, CB_GM_U // 3)),
            pl.BlockSpec((tm, DA_HEADS * LANES), rows),
            pl.BlockSpec((tm, DN_WIDTH), rows),
            pl.BlockSpec((tm, PLE_DIM), rows),
            pl.BlockSpec((GM_GROUPS, GM_CHUNK, GM_CHUNK), lambda i: (0, 0, 0)),
            pl.BlockSpec((GM_CHUNK, GM_GROUPS), const2),
            pl.BlockSpec((GM_WIDTH, D_MODEL), const2),
            pl.BlockSpec((DA_HEADS * LANES, D_MODEL), const2),
            pl.BlockSpec((DN_WIDTH, D_MODEL), const2),
            pl.BlockSpec((D_MODEL, D_MODEL), const2),
            pl.BlockSpec((1, D_MODEL), const2),
            pl.BlockSpec((D_MODEL, D_MODEL), const2),
            pl.BlockSpec((PLE_DIM, D_MODEL), const2),
            pl.BlockSpec((1, D_MODEL), const2),
        ],
        out_specs=pl.BlockSpec((tm, D_MODEL), rows),
        out_shape=jax.ShapeDtypeStruct((t, D_MODEL), F32),
        scratch_shapes=[pltpu.VMEM((tm, GM_WIDTH), BF16)],
        compiler_params=pltpu.CompilerParams(
            dimension_semantics=("arbitrary",), vmem_limit_bytes=VMEM_LIMIT),
        name="merge",
    )(x2, proj, proj, yb, yc, p2, ws, bst, wa, wb, wc, wo, png, wpg, wpp, fng)


def _permute_w_in(w):
    o_gm, o_da, o_dn = 0, 3 * GM_WIDTH, 3 * GM_WIDTH + 4 * 512
    o_ab = o_dn + 3 * DN_WIDTH
    o_dnz = o_ab + 2 * DN_HEADS
    o_gates = o_dnz + DN_WIDTH
    sl = lambda a, n: w[..., a:a + n]
    main = jnp.concatenate([
        sl(o_gates, 3 * D_MODEL),
        sl(o_gm, 3 * GM_WIDTH),
        sl(o_dn, 3 * DN_WIDTH),
        sl(o_da, 4 * 512),
        sl(o_dnz, DN_WIDTH),
    ], axis=-1).astype(BF16)
    ab = jnp.pad(sl(o_ab, 2 * DN_HEADS), ((0, 0), (0, 0), (0, LANES - 2 * DN_HEADS))).astype(BF16)
    return main, ab


def _rotary_tables(positions):
    half = ROPE_DIM // 2
    inv_freq = ROPE_THETA ** (-jnp.arange(0, ROPE_DIM, 2, dtype=F32) / ROPE_DIM)
    ang = positions.astype(F32)[..., None] * inv_freq
    cos, sin = jnp.cos(ang), jnp.sin(ang)
    rest = DA_HEAD_DIM - ROPE_DIM
    tc = jnp.concatenate([cos, cos, jnp.ones(ang.shape[:-1] + (rest,), F32)], axis=-1)
    ts = jnp.concatenate([-sin, sin, jnp.zeros(ang.shape[:-1] + (rest,), F32)], axis=-1)
    rep = LANES // DA_HEAD_DIM
    tc = jnp.tile(tc, (1, 1, rep)).reshape(-1, LANES)
    ts = jnp.tile(ts, (1, 1, rep)).reshape(-1, LANES)
    return tc, ts


def _row(v, width=None):
    v = v.reshape(1, -1).astype(F32)
    if width is not None and v.shape[1] < width:
        v = jnp.pad(v, ((0, 0), (0, width - v.shape[1])))
    return v


def kernel(x, p, positions, norm_g, w_in, gm_ln_g, gm_ln_b, gm_ws, gm_bs, da_lq1, da_lk1, da_lq2, da_lk2, da_subln_g, dn_conv_w, dn_a_log, dn_dt_bias, dn_norm_g, w_br_a, w_br_b, w_br_c, w_out, ple_norm_g, w_ple_gate, w_ple_proj, final_norm_g):
    batch, seq, _ = x.shape
    depth = w_in.shape[0]
    t = batch * seq
    assert seq % DN_CHUNK_ROWS == 0 and seq % 512 == 0

    w_main, w_ab = _permute_w_in(w_in)
    tabc, tabs = _rotary_tables(positions)
    wa16, wb16, wc16 = w_br_a.astype(BF16), w_br_b.astype(BF16), w_br_c.astype(BF16)
    wo16, wpg16, wpp16 = w_out.astype(BF16), w_ple_gate.astype(BF16), w_ple_proj.astype(BF16)
    fng = _row(final_norm_g)

    xc = x.reshape(t, D_MODEL)
    for i in range(depth):
        lambda_init = 0.8 - 0.6 * math.exp(-0.3 * i)
        proj, ab = _inproj(
            xc, _row(norm_g[i]), w_main[i], w_ab[i], _row(gm_ln_g[i]), _row(gm_ln_b[i]), tabc, tabs,
            _row(dn_a_log[i], LANES), _row(dn_dt_bias[i], LANES), seq=seq)
        lam_p = jnp.pad(jnp.stack([da_lq1[i], da_lk1[i], da_lq2[i], da_lk2[i]]).astype(F32),
                        ((0, 4), (0, LANES - DA_HEAD_DIM)))
        yb = _attn(proj, lam_p, _row(da_subln_g[i]), batch=batch, seq=seq, lambda_init=lambda_init)
        abt = jnp.swapaxes(ab.reshape(batch, seq, LANES)[:, :, :8], 1, 2)
        yc = _gdn(proj, ab, abt, dn_conv_w[i].astype(F32), _row(dn_norm_g[i]), batch=batch, seq=seq)
        xc = _merge(xc, proj, yb, yc, p[i].reshape(t, PLE_DIM), gm_ws[i].astype(F32),
                    gm_bs[i].T.astype(F32), wa16[i], wb16[i], wc16[i], wo16[i],
                    _row(ple_norm_g[i]), wpg16[i], wpp16[i], fng, final=(i == depth - 1))
    return xc.reshape(batch, seq, D_MODEL)
```

```python
import functools
import math

import jax
import jax.numpy as jnp
import numpy as np
from jax import lax
from jax.experimental import pallas as pl
from jax.experimental.pallas import tpu as pltpu

F32 = jnp.float32
BF16 = jnp.bfloat16

D_MODEL = 1024
PLE_DIM = 256
EPS = 1e-6
GM_GROUPS = 4
GM_CHUNK = 128
GM_WIDTH = 512
DA_HEADS = 4
DA_HEAD_DIM = 64
ROPE_THETA = 500000.0
ROPE_DIM = 16
DN_HEADS = 4
DN_HEAD_DIM = 128
DN_WIDTH = 512
DN_CONV = 4

LANES = 128
COL_BLOCK = 512
N_COL_BLOCKS = 17
CB_GATES = 0
CB_GM_U, CB_GM_V, CB_GM_Z = 6, 7, 8
CB_DN_QKV = 9
CB_DA_Q, CB_DA_K, CB_DA_V, CB_DA_Z = 12, 13, 14, 15
CB_DN_Z = 16
VMEM_LIMIT = 56 * 1024 * 1024
DN_CHUNK_ROWS = 256
CONV_PAD = 8


def _sigmoid(x):
    return 0.5 * jnp.tanh(0.5 * x) + 0.5


def _gelu_tanh(x):
    c = math.sqrt(2.0 / math.pi)
    return 0.5 * x * (1.0 + jnp.tanh(c * (x + 0.044715 * (x * x * x))))


def _nt_dot(a, b):
    return lax.dot_general(a, b, (((1,), (1,)), ((), ())), preferred_element_type=F32)


def _inproj_kernel(x_ref, ng_ref, w_ref, wab_ref, lng_ref, lnb_ref, tc_ref, ts_ref,
                   alog_ref, dtb_ref, proj_ref, ab_ref, h_ref, *, rc, rcm):
    j = pl.program_id(1)
    tm = x_ref.shape[0]
    nchunk = tm // rc

    @pl.when(j == 0)
    def _():
        def body(c, carry):
            r = pl.ds(pl.multiple_of(c * rc, rc), rc)
            xv = x_ref[r, :]
            ms = jnp.mean(xv * xv, axis=-1, keepdims=True)
            hb = (xv * lax.rsqrt(ms + EPS) * ng_ref[...]).astype(BF16)
            h_ref[r, :] = hb
            z = jnp.dot(hb, wab_ref[...], preferred_element_type=F32)
            lane = lax.broadcasted_iota(jnp.int32, z.shape, 1)
            zz = z + dtb_ref[...]
            sp = jnp.maximum(zz, 0.0) + jnp.log1p(jnp.exp(-jnp.abs(zz)))
            gval = -jnp.exp(alog_ref[...]) * sp
            ab_ref[r, :] = jnp.where(lane < DN_HEADS, gval, _sigmoid(z))
            return carry
        lax.fori_loop(0, nchunk, body, 0)

    def ep_sig(r, acc):
        proj_ref[r, :] = _sigmoid(acc).astype(BF16)

    def ep_gelu(r, acc):
        proj_ref[r, :] = _gelu_tanh(acc).astype(BF16)

    def ep_geln(r, acc):
        g = _gelu_tanh(acc)
        mu = jnp.mean(g, axis=-1, keepdims=True)
        gc = g - mu
        var = jnp.mean(gc * gc, axis=-1, keepdims=True)
        y = gc * lax.rsqrt(var + EPS) * lng_ref[...] + lnb_ref[...]
        proj_ref[r, :] = y.astype(BF16)

    def ep_silu(r, acc):
        proj_ref[r, :] = (acc * _sigmoid(acc)).astype(BF16)

    def ep_id(r, acc):
        proj_ref[r, :] = acc.astype(BF16)

    def ep_rot(r, acc):
        scale = jnp.where(j == CB_DA_Q, DA_HEAD_DIM ** -0.5, 1.0).astype(F32)
        tcv = tc_ref[r, :]
        tsv = ts_ref[r, :]
        lane = lax.broadcasted_iota(jnp.int32, tcv.shape, 1)
        first_half = (lane & (DA_HEAD_DIM - 1)) < (ROPE_DIM // 2)
        for s in range(COL_BLOCK // LANES):
            a = acc[:, s * LANES:(s + 1) * LANES]
            up = pltpu.roll(a, LANES - ROPE_DIM // 2, 1)
            dn = pltpu.roll(a, ROPE_DIM // 2, 1)
            o = a * tcv + jnp.where(first_half, up, dn) * tsv
            proj_ref[r, s * LANES:(s + 1) * LANES] = (o * scale).astype(BF16)

    kinds = (
        (j < CB_GM_U, ep_sig),
        (j == CB_GM_U, ep_gelu),
        (j == CB_GM_V, ep_geln),
        ((j == CB_GM_Z) | (j == CB_DA_Z) | (j == CB_DN_Z), ep_silu),
        (((j >= CB_DN_QKV) & (j < CB_DA_Q)) | (j == CB_DA_V), ep_id),
        ((j == CB_DA_Q) | (j == CB_DA_K), ep_rot),
    )
    for cond, epilogue in kinds:
        @pl.when(cond)
        def _(epilogue=epilogue):
            for c in range(tm // rcm):
                r = pl.ds(c * rcm, rcm)
                acc = jnp.dot(h_ref[r, :], w_ref[...], preferred_element_type=F32)
                epilogue(r, acc)


def _inproj(x2, ng, w, wab, lng, lnb, tabc, tabs, alog, dtb, *, seq):
    t = x2.shape[0]
    tm = seq
    rc = 256
    grid = (t // tm, N_COL_BLOCKS)
    const = lambda i, j: (0, 0)
    rows = lambda i, j: (i, 0)
    return pl.pallas_call(
        functools.partial(_inproj_kernel, rc=rc, rcm=min(tm, 512)),
        grid=grid,
        in_specs=[
            pl.BlockSpec((tm, D_MODEL), rows),
            pl.BlockSpec((1, D_MODEL), const),
            pl.BlockSpec((D_MODEL, COL_BLOCK), lambda i, j: (0, j)),
            pl.BlockSpec((D_MODEL, LANES), const),
            pl.BlockSpec((1, COL_BLOCK), const),
            pl.BlockSpec((1, COL_BLOCK), const),
            pl.BlockSpec((tm, LANES), rows),
            pl.BlockSpec((tm, LANES), rows),
            pl.BlockSpec((1, LANES), const),
            pl.BlockSpec((1, LANES), const),
        ],
        out_specs=[
            pl.BlockSpec((tm, COL_BLOCK), lambda i, j: (i, j)),
            pl.BlockSpec((tm, LANES), rows),
        ],
        out_shape=[
            jax.ShapeDtypeStruct((t, N_COL_BLOCKS * COL_BLOCK), BF16),
            jax.ShapeDtypeStruct((t, LANES), F32),
        ],
        scratch_shapes=[pltpu.VMEM((tm, D_MODEL), BF16)],
        compiler_params=pltpu.CompilerParams(
            dimension_semantics=("arbitrary", "arbitrary"), vmem_limit_bytes=VMEM_LIMIT),
        name="inproj",
    )(x2, ng, w, wab, lng, lnb, tabc, tabs, alog, dtb)


def _attn_kernel(q_ref, k_ref, v_ref, z_ref, lam_ref, sg_ref, o_ref, vt_ref, *, lambda_init, tq):
    seq = q_ref.shape[0]
    tk = tq
    nq = seq // tq

    lp = lam_ref[...]
    lam = (jnp.exp(jnp.sum(lp[0:1] * lp[1:2], axis=-1, keepdims=True))
           - jnp.exp(jnp.sum(lp[2:3] * lp[3:4], axis=-1, keepdims=True)) + lambda_init)

    for c0 in range(0, seq, tk):
        vt_ref[:, c0:c0 + tk] = v_ref[c0:c0 + tk, :].astype(F32).T.astype(BF16)

    def qblock(qi, carry0):
        qr = pl.ds(pl.multiple_of(qi * tq, tq), tq)
        q = q_ref[qr, :]
        lane = lax.broadcasted_iota(jnp.int32, q.shape, 1)
        zero = jnp.zeros_like(q)
        qs = (jnp.where(lane < DA_HEAD_DIM, q, zero), jnp.where(lane >= DA_HEAD_DIM, q, zero))

        def block(kb, carry, masked):
            kr = pl.ds(pl.multiple_of(kb * tk, tk), tk)
            ks = k_ref[kr, :]
            vts = vt_ref[:, kr]
            sts = [_nt_dot(ks, qs[c]) for c in range(2)]
            out = []
            for c in range(2):
                m, l, a = carry[c]
                st = sts[c]
                if masked:
                    ri = lax.broadcasted_iota(jnp.int32, st.shape, 0)
                    ci = lax.broadcasted_iota(jnp.int32, st.shape, 1)
                    st = jnp.where(ri <= ci, st, -jnp.inf)
                mn = jnp.maximum(m, jnp.max(st, axis=0, keepdims=True))
                alpha = jnp.exp(m - mn)
                pt = jnp.exp(st - mn)
                l = alpha * l + jnp.sum(pt, axis=0, keepdims=True)
                a = alpha * a + jnp.dot(vts, pt.astype(BF16), preferred_element_type=F32)
                out.append((mn, l, a))
            return tuple(out)

        init = tuple((jnp.full((1, tq), -1e30, F32), jnp.zeros((1, tq), F32),
                      jnp.zeros((LANES, tq), F32)) for _ in range(2))
        carry = lax.fori_loop(0, qi, lambda kb, cr: block(kb, cr, False), init)
        (_, l0, a0), (_, l1, a1) = block(qi, carry, True)

        ot = a0 / l0 - lam * (a1 / l1)
        ot = ot * lax.rsqrt(jnp.mean(ot * ot, axis=0, keepdims=True) + EPS)
        o = ot.T * sg_ref[...] * (1.0 - lambda_init)
        o_ref[qr, :] = (o * z_ref[qr, :].astype(F32)).astype(BF16)
        return carry0

    lax.fori_loop(0, nq, qblock, 0)


def _attn(proj, lam_p, subln_g, *, batch, seq, lambda_init):
    t = proj.shape[0]
    per = COL_BLOCK // LANES
    col = lambda cb: (lambda b, h: (b, cb * per + h))
    return pl.pallas_call(
        functools.partial(_attn_kernel, lambda_init=lambda_init, tq=min(seq, 512)),
        grid=(batch, DA_HEADS),
        in_specs=[
            pl.BlockSpec((seq, LANES), col(CB_DA_Q)),
            pl.BlockSpec((seq, LANES), col(CB_DA_K)),
            pl.BlockSpec((seq, LANES), col(CB_DA_V)),
            pl.BlockSpec((seq, LANES), col(CB_DA_Z)),
            pl.BlockSpec((8, LANES), lambda b, h: (0, 0)),
            pl.BlockSpec((1, LANES), lambda b, h: (0, 0)),
        ],
        out_specs=pl.BlockSpec((seq, LANES), lambda b, h: (b, h)),
        out_shape=jax.ShapeDtypeStruct((t, DA_HEADS * LANES), BF16),
        scratch_shapes=[pltpu.VMEM((LANES, seq), BF16)],
        compiler_params=pltpu.CompilerParams(
            dimension_semantics=("arbitrary", "arbitrary"), vmem_limit_bytes=VMEM_LIMIT),
        name="attn",
    )(proj, proj, proj, proj, lam_p, subln_g)


def _segment_cumsum(x, axis, seg):
    idx = lax.broadcasted_iota(jnp.int32, x.shape, axis) & (seg - 1)
    s = 1
    while s < seg:
        x = x + jnp.where(idx >= s, pltpu.roll(x, s, axis), 0.0)
        s *= 2
    return x


def _gdn_kernel(qkv_ref, z_ref, ab_ref, abt_ref, cw_ref, ng_ref, o_ref, halo_ref, st_ref):
    seq = qkv_ref.shape[0]
    cr = DN_CHUNK_ROWS
    ngroups = seq // cr
    nh = DN_HEADS

    halo_ref[...] = jnp.zeros(halo_ref.shape, F32)
    st_ref[...] = jnp.zeros(st_ref.shape, F32)

    def conv_silu(r, part, hh):
        col = part * DN_WIDTH + hh * LANES
        idx = part * nh + hh
        cur = qkv_ref[r, col:col + LANES].astype(F32)
        xa = jnp.concatenate([halo_ref[idx], cur], axis=0)
        halo_ref[idx] = cur[cr - CONV_PAD:, :]
        w = cw_ref[:, col:col + LANES]
        y = cur * w[DN_CONV - 1:DN_CONV, :]
        for tap in range(DN_CONV - 1):
            lo = CONV_PAD - (DN_CONV - 1 - tap)
            y = y + xa[lo:lo + cr, :] * w[tap:tap + 1, :]
        return y * _sigmoid(y)

    def l2n(y):
        return y * lax.rsqrt(jnp.sum(y * y, axis=-1, keepdims=True) + EPS)

    def group(g, carry):
        r = pl.ds(pl.multiple_of(g * cr, cr), cr)
        abv = ab_ref[r, :]
        gcol = _segment_cumsum(abv, 0, cr)
        grow = _segment_cumsum(abt_ref[:, r], 1, cr)
        ri = lax.broadcasted_iota(jnp.int32, (cr, cr), 0)
        ci = lax.broadcasted_iota(jnp.int32, (cr, cr), 1)
        causal = ri >= ci
        strict = ri > ci
        heads = range(nh)
        dot = functools.partial(jnp.dot, preferred_element_type=F32)
        gcc = [gcol[:, hh:hh + 1] for hh in heads]
        bcol = [abv[:, nh + hh:nh + hh + 1] for hh in heads]
        decay = [jnp.exp(jnp.where(causal, gcc[hh] - grow[hh:hh + 1, :], -jnp.inf)) for hh in heads]
        qf = [l2n(conv_silu(r, 0, hh)) * (DN_HEAD_DIM ** -0.5) for hh in heads]
        kf = [l2n(conv_silu(r, 1, hh)) for hh in heads]
        vf = [conv_silu(r, 2, hh) for hh in heads]
        k16 = [kf[hh].astype(BF16) for hh in heads]
        a = [jnp.where(strict, (bcol[hh] * _nt_dot(k16[hh], k16[hh])) * decay[hh], 0.0) for hh in heads]
        n = [-a[hh] for hh in heads]
        pw = a
        for _ in range(int(math.log2(cr)) - 1):
            pw16 = [pw[hh].astype(BF16) for hh in heads]
            pw = [dot(pw16[hh], pw16[hh]) for hh in heads]
            n = [n[hh] + pw[hh] + dot(n[hh].astype(BF16), pw[hh].astype(BF16)) for hh in heads]
        eg = [jnp.exp(gcc[hh]) for hh in heads]
        rhs = [jnp.concatenate([vf[hh] * bcol[hh], kf[hh] * (bcol[hh] * eg[hh])], axis=1) for hh in heads]
        uw = [rhs[hh] + dot(n[hh].astype(BF16), rhs[hh].astype(BF16)) for hh in heads]
        qk = [_nt_dot(qf[hh].astype(BF16), k16[hh]) * decay[hh] for hh in heads]
        gl = [gcc[hh][cr - 1:cr, :] for hh in heads]
        kdt = [(kf[hh] * jnp.exp(gl[hh] - gcc[hh])).T.astype(BF16) for hh in heads]
        st = [st_ref[hh] for hh in heads]
        st16 = [st[hh].astype(BF16) for hh in heads]
        v16 = [(uw[hh][:, :LANES] - dot(uw[hh][:, LANES:].astype(BF16), st16[hh])).astype(BF16) for hh in heads]
        o = [dot((qf[hh] * eg[hh]).astype(BF16), st16[hh]) + dot(qk[hh].astype(BF16), v16[hh]) for hh in heads]
        for hh in heads:
            st_ref[hh] = st[hh] * jnp.exp(gl[hh]) + dot(kdt[hh], v16[hh])
            oh = o[hh] * lax.rsqrt(jnp.mean(o[hh] * o[hh], axis=-1, keepdims=True) + EPS) * ng_ref[...]
            zc = z_ref[r, hh * LANES:(hh + 1) * LANES].astype(F32)
            o_ref[r, hh * LANES:(hh + 1) * LANES] = (oh * zc).astype(BF16)
        return carry

    lax.fori_loop(0, ngroups, group, 0)


def _gdn(proj, ab, abt, conv_w, norm_g, *, batch, seq):
    t = proj.shape[0]
    return pl.pallas_call(
        _gdn_kernel,
        grid=(batch,),
        in_specs=[
            pl.BlockSpec((seq, 3 * DN_WIDTH), lambda b: (b, CB_DN_QKV // 3)),
            pl.BlockSpec((seq, DN_WIDTH), lambda b: (b, CB_DN_Z)),
            pl.BlockSpec((seq, LANES), lambda b: (b, 0)),
            pl.BlockSpec((None, 8, seq), lambda b: (b, 0, 0)),
            pl.BlockSpec((DN_CONV, 3 * DN_WIDTH), lambda b: (0, 0)),
            pl.BlockSpec((1, LANES), lambda b: (0, 0)),
        ],
        out_specs=pl.BlockSpec((seq, DN_WIDTH), lambda b: (b, 0)),
        out_shape=jax.ShapeDtypeStruct((t, DN_WIDTH), BF16),
        scratch_shapes=[
            pltpu.VMEM((3 * DN_HEADS, CONV_PAD, LANES), F32),
            pltpu.VMEM((DN_HEADS, DN_HEAD_DIM, DN_HEAD_DIM), F32),
        ],
        compiler_params=pltpu.CompilerParams(
            dimension_semantics=("arbitrary",), vmem_limit_bytes=VMEM_LIMIT),
        name="gdn",
    )(proj, proj, ab, abt, conv_w, norm_g)


def _merge_kernel(x_ref, gates_ref, gm_ref, yb_ref, yc_ref, p_ref, ws_ref, bst_ref,
                  wa_ref, wb_ref, wc_ref, wo_ref, png_ref, wpg_ref, wpp_ref, fng_ref,
                  o_ref, ya_ref, *, rc, final):
    tm = x_ref.shape[0]

    ri = lax.broadcasted_iota(jnp.int32, (GM_CHUNK, GM_CHUNK), 0)
    ci = lax.broadcasted_iota(jnp.int32, (GM_CHUNK, GM_CHUNK), 1)
    wtri = [jnp.where(ri >= ci, ws_ref[g], 0.0).astype(BF16) for g in range(GM_GROUPS)]

    def gm_body(c, carry):
        r = pl.ds(pl.multiple_of(c * GM_CHUNK, GM_CHUNK), GM_CHUNK)
        for g in range(GM_GROUPS):
            cols = slice(g * LANES, (g + 1) * LANES)
            u = gm_ref[r, cols].astype(F32)
            v = gm_ref[r, GM_WIDTH + g * LANES:GM_WIDTH + (g + 1) * LANES]
            z = gm_ref[r, 2 * GM_WIDTH + g * LANES:2 * GM_WIDTH + (g + 1) * LANES].astype(F32)
            mix = jnp.dot(wtri[g], v, preferred_element_type=F32) + bst_ref[:, g:g + 1]
            ya_ref[r, cols] = (u * mix * z).astype(BF16)
        return carry
    lax.fori_loop(0, tm // GM_CHUNK, gm_body, 0)

    def body(c, carry):
        r = pl.ds(pl.multiple_of(c * rc, rc), rc)
        ga = gates_ref[r, 0:D_MODEL].astype(F32)
        gb = gates_ref[r, D_MODEL:2 * D_MODEL].astype(F32)
        gc = gates_ref[r, 2 * D_MODEL:3 * D_MODEL].astype(F32)
        m = (ga * jnp.dot(ya_ref[r, :], wa_ref[...], preferred_element_type=F32)
             + gb * jnp.dot(yb_ref[r, :], wb_ref[...], preferred_element_type=F32)
             + gc * jnp.dot(yc_ref[r, :], wc_ref[...], preferred_element_type=F32))
        x1 = x_ref[r, :] + jnp.dot(m.astype(BF16), wo_ref[...], preferred_element_type=F32)
        hn = x1 * lax.rsqrt(jnp.mean(x1 * x1, axis=-1, keepdims=True) + EPS) * png_ref[...]
        gate = _sigmoid(jnp.dot(hn.astype(BF16), wpg_ref[...], preferred_element_type=F32))
        pp = jnp.dot(p_ref[r, :].astype(BF16), wpp_ref[...], preferred_element_type=F32)
        x2 = x1 + gate * pp
        if final:
            x2 = x2 * lax.rsqrt(jnp.mean(x2 * x2, axis=-1, keepdims=True) + EPS) * fng_ref[...]
        o_ref[r, :] = x2
        return carry
    lax.fori_loop(0, tm // rc, body, 0)


def _merge(x2, proj, yb, yc, p2, ws, bst, wa, wb, wc, wo, png, wpg, wpp, fng, *, final):
    t = x2.shape[0]
    tm = 512
    rc = 256
    rows = lambda i: (i, 0)
    const2 = lambda i: (0, 0)
    return pl.pallas_call(
        functools.partial(_merge_kernel, rc=rc, final=final),
        grid=(t // tm,),
        in_specs=[
            pl.BlockSpec((tm, D_MODEL), rows),
            pl.BlockSpec((tm, 3 * D_MODEL), lambda i: (i, CB_GATES)),
            pl.BlockSpec((tm, 3 * GM_WIDTH), lambda i: (i, CB_GM_U // 3)),
            pl.BlockSpec((tm, DA_HEADS * LANES), rows),
            pl.BlockSpec((tm, DN_WIDTH), rows),
            pl.BlockSpec((tm, PLE_DIM), rows),
            pl.BlockSpec((GM_GROUPS, GM_CHUNK, GM_CHUNK), lambda i: (0, 0, 0)),
            pl.BlockSpec((GM_CHUNK, GM_GROUPS), const2),
            pl.BlockSpec((GM_WIDTH, D_MODEL), const2),
            pl.BlockSpec((DA_HEADS * LANES, D_MODEL), const2),
            pl.BlockSpec((DN_WIDTH, D_MODEL), const2),
            pl.BlockSpec((D_MODEL, D_MODEL), const2),
            pl.BlockSpec((1, D_MODEL), const2),
            pl.BlockSpec((D_MODEL, D_MODEL), const2),
            pl.BlockSpec((PLE_DIM, D_MODEL), const2),
            pl.BlockSpec((1, D_MODEL), const2),
        ],
        out_specs=pl.BlockSpec((tm, D_MODEL), rows),
        out_shape=jax.ShapeDtypeStruct((t, D_MODEL), F32),
        scratch_shapes=[pltpu.VMEM((tm, GM_WIDTH), BF16)],
        compiler_params=pltpu.CompilerParams(
            dimension_semantics=("arbitrary",), vmem_limit_bytes=VMEM_LIMIT),
        name="merge",
    )(x2, proj, proj, yb, yc, p2, ws, bst, wa, wb, wc, wo, png, wpg, wpp, fng)


def _permute_w_in(w):
    o_gm, o_da, o_dn = 0, 3 * GM_WIDTH, 3 * GM_WIDTH + 4 * 512
    o_ab = o_dn + 3 * DN_WIDTH
    o_dnz = o_ab + 2 * DN_HEADS
    o_gates = o_dnz + DN_WIDTH
    sl = lambda a, n: w[..., a:a + n]
    main = jnp.concatenate([
        sl(o_gates, 3 * D_MODEL),
        sl(o_gm, 3 * GM_WIDTH),
        sl(o_dn, 3 * DN_WIDTH),
        sl(o_da, 4 * 512),
        sl(o_dnz, DN_WIDTH),
    ], axis=-1).astype(BF16)
    ab = jnp.pad(sl(o_ab, 2 * DN_HEADS), ((0, 0), (0, 0), (0, LANES - 2 * DN_HEADS))).astype(BF16)
    return main, ab


def _rotary_tables(positions):
    half = ROPE_DIM // 2
    inv_freq = ROPE_THETA ** (-jnp.arange(0, ROPE_DIM, 2, dtype=F32) / ROPE_DIM)
    ang = positions.astype(F32)[..., None] * inv_freq
    cos, sin = jnp.cos(ang), jnp.sin(ang)
    rest = DA_HEAD_DIM - ROPE_DIM
    tc = jnp.concatenate([cos, cos, jnp.ones(ang.shape[:-1] + (rest,), F32)], axis=-1)
    ts = jnp.concatenate([-sin, sin, jnp.zeros(ang.shape[:-1] + (rest,), F32)], axis=-1)
    rep = LANES // DA_HEAD_DIM
    tc = jnp.tile(tc, (1, 1, rep)).reshape(-1, LANES)
    ts = jnp.tile(ts, (1, 1, rep)).reshape(-1, LANES)
    return tc, ts


def _row(v, width=None):
    v = v.reshape(1, -1).astype(F32)
    if width is not None and v.shape[1] < width:
        v = jnp.pad(v, ((0, 0), (0, width - v.shape[1])))
    return v


def kernel(x, p, positions, norm_g, w_in, gm_ln_g, gm_ln_b, gm_ws, gm_bs, da_lq1, da_lk1, da_lq2, da_lk2, da_subln_g, dn_conv_w, dn_a_log, dn_dt_bias, dn_norm_g, w_br_a, w_br_b, w_br_c, w_out, ple_norm_g, w_ple_gate, w_ple_proj, final_norm_g):
    batch, seq, _ = x.shape
    depth = w_in.shape[0]
    t = batch * seq
    assert seq % DN_CHUNK_ROWS == 0 and seq % 512 == 0

    w_main, w_ab = _permute_w_in(w_in)
    tabc, tabs = _rotary_tables(positions)
    wa16, wb16, wc16 = w_br_a.astype(BF16), w_br_b.astype(BF16), w_br_c.astype(BF16)
    wo16, wpg16, wpp16 = w_out.astype(BF16), w_ple_gate.astype(BF16), w_ple_proj.astype(BF16)
    fng = _row(final_norm_g)

    xc = x.reshape(t, D_MODEL)
    for i in range(depth):
        lambda_init = 0.8 - 0.6 * math.exp(-0.3 * i)
        proj, ab = _inproj(
            xc, _row(norm_g[i]), w_main[i], w_ab[i], _row(gm_ln_g[i]), _row(gm_ln_b[i]), tabc, tabs,
            _row(dn_a_log[i], LANES), _row(dn_dt_bias[i], LANES), seq=seq)
        lam_p = jnp.pad(jnp.stack([da_lq1[i], da_lk1[i], da_lq2[i], da_lk2[i]]).astype(F32),
                        ((0, 4), (0, LANES - DA_HEAD_DIM)))
        yb = _attn(proj, lam_p, _row(da_subln_g[i]), batch=batch, seq=seq, lambda_init=lambda_init)
        abt = jnp.swapaxes(ab.reshape(batch, seq, LANES)[:, :, :8], 1, 2)
        yc = _gdn(proj, ab, abt, dn_conv_w[i].astype(F32), _row(dn_norm_g[i]), batch=batch, seq=seq)
        xc = _merge(xc, proj, yb, yc, p[i].reshape(t, PLE_DIM), gm_ws[i].astype(F32),
                    gm_bs[i].T.astype(F32), wa16[i], wb16[i], wc16[i], wo16[i],
                    _row(ple_norm_g[i]), wpg16[i], wpp16[i], fng, final=(i == depth - 1))
    return xc.reshape(batch, seq, D_MODEL)
```

```python
import functools
import math

import jax
import jax.numpy as jnp
from jax import lax
from jax.experimental import pallas as pl
from jax.experimental.pallas import tpu as pltpu

F32 = jnp.float32
BF16 = jnp.bfloat16

D_MODEL = 1024
PLE_DIM = 256
EPS = 1e-6
GM_GROUPS = 4
GM_CHUNK = 128
GM_WIDTH = 512
DA_HEADS = 4
DA_HEAD_DIM = 64
ROPE_THETA = 500000.0
ROPE_DIM = 16
DN_HEADS = 4
DN_HEAD_DIM = 128
DN_WIDTH = 512
DN_CONV = 4

LANES = 128
COL_BLOCK = 512
N_COL_BLOCKS = 17
CB_GATES = 0
CB_GM_U, CB_GM_V, CB_GM_Z = 6, 7, 8
CB_DN_QKV = 9
CB_DA_Q, CB_DA_K, CB_DA_V, CB_DA_Z = 12, 13, 14, 15
CB_DN_Z = 16
VMEM_LIMIT = 56 * 1024 * 1024
DN_CHUNK_ROWS = 256
CONV_PAD = 8
LOG2_E = math.log2(math.e)
INPROJ_ROWS = 512
INPROJ_ORDER = (9, 0, 10, 1, 11, 2, 7, 3, 6, 4, 12, 5, 13, 14, 8, 15, 16)


def _sigmoid(x):
    return 0.5 * jnp.tanh(0.5 * x) + 0.5


def _gelu_tanh(x):
    c = math.sqrt(2.0 / math.pi)
    return 0.5 * x * (1.0 + jnp.tanh(c * (x + 0.044715 * (x * x * x))))


def _nt_dot(a, b):
    return lax.dot_general(a, b, (((1,), (1,)), ((), ())), preferred_element_type=F32)


def _inproj_kernel(x_ref, ng_ref, w_ref, wab_ref, lng_ref, lnb_ref, tc_ref, ts_ref,
                   alog_ref, dtb_ref, cw_ref, proj_ref, ab_ref, h_ref, tail_ref, *, tiles_per_seq):
    tm = x_ref.shape[0]

    xv = x_ref[...]
    ms = jnp.mean(xv * xv, axis=-1, keepdims=True)
    hb = (xv * lax.rsqrt(ms + EPS) * ng_ref[...]).astype(BF16)
    h_ref[...] = hb
    z = jnp.dot(hb, wab_ref[...], preferred_element_type=F32)
    lane = lax.broadcasted_iota(jnp.int32, z.shape, 1)
    zz = z + dtb_ref[...]
    sp = jnp.maximum(zz, 0.0) + jnp.log1p(jnp.exp(-jnp.abs(zz)))
    ab_ref[...] = jnp.where(lane < DN_HEADS, -jnp.exp(alog_ref[...]) * sp, _sigmoid(z))

    @pl.when(pl.program_id(0) % tiles_per_seq == 0)
    def _():
        tail_ref[...] = jnp.zeros(tail_ref.shape, F32)

    def ep_sig(j, acc):
        return _sigmoid(acc)

    def ep_gelu(j, acc):
        return _gelu_tanh(acc)

    def ep_geln(j, acc):
        g = _gelu_tanh(acc)
        mu = jnp.mean(g, axis=-1, keepdims=True)
        gc = g - mu
        var = jnp.mean(gc * gc, axis=-1, keepdims=True)
        return gc * lax.rsqrt(var + EPS) * lng_ref[...] + lnb_ref[...]

    def ep_silu(j, acc):
        return acc * _sigmoid(acc)

    def ep_id(j, acc):
        return acc

    def ep_rot(j, acc):
        scale = DA_HEAD_DIM ** -0.5 * LOG2_E if j == CB_DA_Q else 1.0
        tcv = tc_ref[...] * scale
        tsv = ts_ref[...] * scale
        lane = lax.broadcasted_iota(jnp.int32, tcv.shape, 1)
        first_half = (lane & (DA_HEAD_DIM - 1)) < (ROPE_DIM // 2)
        outs = []
        for s in range(COL_BLOCK // LANES):
            a = acc[:, s * LANES:(s + 1) * LANES]
            up = pltpu.roll(a, LANES - ROPE_DIM // 2, 1)
            dn = pltpu.roll(a, ROPE_DIM // 2, 1)
            outs.append(a * tcv + jnp.where(first_half, up, dn) * tsv)
        return jnp.concatenate(outs, axis=1)

    def ep_dn(j, acc):
        part = j - CB_DN_QKV
        xa = jnp.concatenate([tail_ref[part], acc], axis=0)
        tail_ref[part] = acc[tm - CONV_PAD:, :]
        w = cw_ref[:, part * COL_BLOCK:(part + 1) * COL_BLOCK]
        y = acc * w[DN_CONV - 1:DN_CONV, :]
        for tap in range(DN_CONV - 1):
            lo = CONV_PAD - (DN_CONV - 1 - tap)
            y = y + xa[lo:lo + tm, :] * w[tap:tap + 1, :]
        y = y * _sigmoid(y)
        if part == 2:
            return y
        scale = DN_HEAD_DIM ** -0.5 if part == 0 else 1.0
        outs = []
        for s in range(COL_BLOCK // LANES):
            ys = y[:, s * LANES:(s + 1) * LANES]
            outs.append(ys * (lax.rsqrt(jnp.sum(ys * ys, axis=-1, keepdims=True) + EPS) * scale))
        return jnp.concatenate(outs, axis=1)

    kind = {CB_GM_U: ep_gelu, CB_GM_V: ep_geln, CB_GM_Z: ep_silu, CB_DA_Z: ep_silu, CB_DN_Z: ep_silu,
            CB_DA_V: ep_id, CB_DA_Q: ep_rot, CB_DA_K: ep_rot}
    kind.update({CB_GATES + g: ep_sig for g in range(6)})
    kind.update({CB_DN_QKV + g: ep_dn for g in range(3)})
    def project(j):
        cols = slice(j * COL_BLOCK, (j + 1) * COL_BLOCK)
        return jnp.dot(h_ref[...], w_ref[:, cols], preferred_element_type=F32)

    acc = project(INPROJ_ORDER[0])
    for idx, j in enumerate(INPROJ_ORDER):
        nxt = project(INPROJ_ORDER[idx + 1]) if idx + 1 < N_COL_BLOCKS else None
        proj_ref[:, j * COL_BLOCK:(j + 1) * COL_BLOCK] = kind[j](j, acc).astype(BF16)
        acc = nxt


def _inproj(x2, ng, w, wab, lng, lnb, tabc, tabs, alog, dtb, conv_w, *, seq):
    t = x2.shape[0]
    tm = INPROJ_ROWS
    ncols = N_COL_BLOCKS * COL_BLOCK
    const = lambda i: (0, 0)
    rows = lambda i: (i, 0)
    return pl.pallas_call(
        functools.partial(_inproj_kernel, tiles_per_seq=seq // tm),
        grid=(t // tm,),
        in_specs=[
            pl.BlockSpec((tm, D_MODEL), rows),
            pl.BlockSpec((1, D_MODEL), const),
            pl.BlockSpec((D_MODEL, ncols), const),
            pl.BlockSpec((D_MODEL, LANES), const),
            pl.BlockSpec((1, COL_BLOCK), const),
            pl.BlockSpec((1, COL_BLOCK), const),
            pl.BlockSpec((tm, LANES), rows),
            pl.BlockSpec((tm, LANES), rows),
            pl.BlockSpec((1, LANES), const),
            pl.BlockSpec((1, LANES), const),
            pl.BlockSpec((DN_CONV, 3 * DN_WIDTH), const),
        ],
        out_specs=[
            pl.BlockSpec((tm, ncols), rows),
            pl.BlockSpec((tm, LANES), rows),
        ],
        out_shape=[
            jax.ShapeDtypeStruct((t, ncols), BF16),
            jax.ShapeDtypeStruct((t, LANES), F32),
        ],
        scratch_shapes=[pltpu.VMEM((tm, D_MODEL), BF16),
                        pltpu.VMEM((3, CONV_PAD, COL_BLOCK), F32)],
        compiler_params=pltpu.CompilerParams(
            dimension_semantics=("arbitrary",), vmem_limit_bytes=VMEM_LIMIT),
        name="inproj",
    )(x2, ng, w, wab, lng, lnb, tabc, tabs, alog, dtb, conv_w)


def _attn_kernel(q_ref, k_ref, v_ref, z_ref, lam_ref, sg_ref, o_ref, vt_ref, st_ref, *, lambda_init, tq):
    seq = q_ref.shape[0]
    tk = tq
    nq = seq // tq

    lp = lam_ref[...]
    lam = (jnp.exp(jnp.sum(lp[0:1] * lp[1:2], axis=-1, keepdims=True))
           - jnp.exp(jnp.sum(lp[2:3] * lp[3:4], axis=-1, keepdims=True)) + lambda_init)

    for c0 in range(0, seq, tk):
        vt_ref[:, c0:c0 + tk] = v_ref[c0:c0 + tk, :].astype(F32).T.astype(BF16)

    def qblock(qi, carry0):
        qr = pl.ds(pl.multiple_of(qi * tq, tq), tq)
        q = q_ref[qr, :]
        lane = lax.broadcasted_iota(jnp.int32, q.shape, 1)
        zero = jnp.zeros_like(q)
        qs = (jnp.where(lane < DA_HEAD_DIM, q, zero), jnp.where(lane >= DA_HEAD_DIM, q, zero))

        def scores(kb, slot):
            ks = k_ref[pl.ds(pl.multiple_of(kb * tk, tk), tk), :]
            for c in range(2):
                st_ref[slot, c] = _nt_dot(ks, qs[c])

        def softmax_pv(kb, slot, carry, masked):
            vts = vt_ref[:, pl.ds(pl.multiple_of(kb * tk, tk), tk)]
            out = []
            for c in range(2):
                m, l, a = carry[c]
                st = st_ref[slot, c]
                if masked:
                    ri = lax.broadcasted_iota(jnp.int32, st.shape, 0)
                    ci = lax.broadcasted_iota(jnp.int32, st.shape, 1)
                    st = jnp.where(ri <= ci, st, -jnp.inf)
                mn = jnp.maximum(m, jnp.max(st, axis=0, keepdims=True))
                alpha = jnp.exp2(m - mn)
                pt = jnp.exp2(st - mn)
                l = alpha * l + jnp.sum(pt, axis=0, keepdims=True)
                a = alpha * a + jnp.dot(vts, pt.astype(BF16), preferred_element_type=F32)
                out.append((mn, l, a))
            return tuple(out)

        def step(kb, carry):
            slot = kb & 1
            carry = softmax_pv(kb, slot, carry, False)
            scores(kb + 1, 1 - slot)
            return carry

        init = tuple((jnp.full((1, tq), -1e30, F32), jnp.zeros((1, tq), F32),
                      jnp.zeros((LANES, tq), F32)) for _ in range(2))
        scores(0, 0)
        carry = lax.fori_loop(0, qi, step, init)
        (_, l0, a0), (_, l1, a1) = softmax_pv(qi, qi & 1, carry, True)

        ot = a0 / l0 - lam * (a1 / l1)
        ot = ot * lax.rsqrt(jnp.mean(ot * ot, axis=0, keepdims=True) + EPS)
        o = ot.T * sg_ref[...] * (1.0 - lambda_init)
        o_ref[qr, :] = (o * z_ref[qr, :].astype(F32)).astype(BF16)
        return carry0

    lax.fori_loop(0, nq, qblock, 0)


def _attn(proj, lam_p, subln_g, *, batch, seq, lambda_init):
    t = proj.shape[0]
    per = COL_BLOCK // LANES
    tq = min(seq, 512)
    col = lambda cb: (lambda b, h: (b, cb * per + h))
    return pl.pallas_call(
        functools.partial(_attn_kernel, lambda_init=lambda_init, tq=tq),
        grid=(batch, DA_HEADS),
        in_specs=[
            pl.BlockSpec((seq, LANES), col(CB_DA_Q)),
            pl.BlockSpec((seq, LANES), col(CB_DA_K)),
            pl.BlockSpec((seq, LANES), col(CB_DA_V)),
            pl.BlockSpec((seq, LANES), col(CB_DA_Z)),
            pl.BlockSpec((8, LANES), lambda b, h: (0, 0)),
            pl.BlockSpec((1, LANES), lambda b, h: (0, 0)),
        ],
        out_specs=pl.BlockSpec((seq, LANES), lambda b, h: (b, h)),
        out_shape=jax.ShapeDtypeStruct((t, DA_HEADS * LANES), BF16),
        scratch_shapes=[pltpu.VMEM((LANES, seq), BF16),
                        pltpu.VMEM((2, 2, tq, tq), F32)],
        compiler_params=pltpu.CompilerParams(
            dimension_semantics=("arbitrary", "arbitrary"), vmem_limit_bytes=VMEM_LIMIT),
        name="attn",
    )(proj, proj, proj, proj, lam_p, subln_g)


def _segment_cumsum(x, axis, seg):
    idx = lax.broadcasted_iota(jnp.int32, x.shape, axis) & (seg - 1)
    s = 1
    while s < seg:
        x = x + jnp.where(idx >= s, pltpu.roll(x, s, axis), 0.0)
        s *= 2
    return x


def _gdn_kernel(qkv_ref, z_ref, ab_ref, abt_ref, ng_ref, o_ref, st_ref):
    seq = qkv_ref.shape[0]
    cr = DN_CHUNK_ROWS
    ngroups = seq // cr
    nh = DN_HEADS

    st_ref[...] = jnp.zeros(st_ref.shape, F32)

    def load(r, part, hh):
        col = part * DN_WIDTH + hh * LANES
        return qkv_ref[r, col:col + LANES]

    def group(g, carry):
        r = pl.ds(pl.multiple_of(g * cr, cr), cr)
        abv = ab_ref[r, :]
        gcol = _segment_cumsum(abv, 0, cr)
        grow = _segment_cumsum(abt_ref[:, r], 1, cr)
        ri = lax.broadcasted_iota(jnp.int32, (cr, cr), 0)
        ci = lax.broadcasted_iota(jnp.int32, (cr, cr), 1)
        causal = ri >= ci
        strict = ri > ci
        heads = range(nh)
        dot = functools.partial(jnp.dot, preferred_element_type=F32)
        gcc = [gcol[:, hh:hh + 1] for hh in heads]
        bcol = [abv[:, nh + hh:nh + hh + 1] for hh in heads]
        decay = [jnp.exp(jnp.where(causal, gcc[hh] - grow[hh:hh + 1, :], -jnp.inf)) for hh in heads]
        q16 = [load(r, 0, hh) for hh in heads]
        k16 = [load(r, 1, hh) for hh in heads]
        qf = [q16[hh].astype(F32) for hh in heads]
        kf = [k16[hh].astype(F32) for hh in heads]
        vf = [load(r, 2, hh).astype(F32) for hh in heads]
        a = [jnp.where(strict, (bcol[hh] * _nt_dot(k16[hh], k16[hh])) * decay[hh], 0.0) for hh in heads]
        n = [-a[hh] for hh in heads]
        pw16 = [a[hh].astype(BF16) for hh in heads]
        for _ in range(int(math.log2(cr)) - 1):
            pw = [dot(pw16[hh], pw16[hh]) for hh in heads]
            pw16 = [pw[hh].astype(BF16) for hh in heads]
            n = [n[hh] + pw[hh] + dot(n[hh].astype(BF16), pw16[hh]) for hh in heads]
        eg = [jnp.exp(gcc[hh]) for hh in heads]
        rhs = [jnp.concatenate([vf[hh] * bcol[hh], kf[hh] * (bcol[hh] * eg[hh])], axis=1) for hh in heads]
        uw = [rhs[hh] + dot(n[hh].astype(BF16), rhs[hh].astype(BF16)) for hh in heads]
        qk = [_nt_dot(q16[hh], k16[hh]) * decay[hh] for hh in heads]
        gl = [gcc[hh][cr - 1:cr, :] for hh in heads]
        kdt = [(kf[hh] * jnp.exp(gl[hh] - gcc[hh])).T.astype(BF16) for hh in heads]
        st = [st_ref[hh] for hh in heads]
        st16 = [st[hh].astype(BF16) for hh in heads]
        v16 = [(uw[hh][:, :LANES] - dot(uw[hh][:, LANES:].astype(BF16), st16[hh])).astype(BF16) for hh in heads]
        o = [dot((qf[hh] * eg[hh]).astype(BF16), st16[hh]) + dot(qk[hh].astype(BF16), v16[hh]) for hh in heads]
        for hh in heads:
            st_ref[hh] = st[hh] * jnp.exp(gl[hh]) + dot(kdt[hh], v16[hh])
            oh = o[hh] * lax.rsqrt(jnp.mean(o[hh] * o[hh], axis=-1, keepdims=True) + EPS) * ng_ref[...]
            zc = z_ref[r, hh * LANES:(hh + 1) * LANES].astype(F32)
            o_ref[r, hh * LANES:(hh + 1) * LANES] = (oh * zc).astype(BF16)
        return carry

    lax.fori_loop(0, ngroups, group, 0)


def _gdn(proj, ab, abt, norm_g, *, batch, seq):
    t = proj.shape[0]
    return pl.pallas_call(
        _gdn_kernel,
        grid=(batch,),
        in_specs=[
            pl.BlockSpec((seq, 3 * DN_WIDTH), lambda b: (b, CB_DN_QKV // 3)),
            pl.BlockSpec((seq, DN_WIDTH), lambda b: (b, CB_DN_Z)),
            pl.BlockSpec((seq, LANES), lambda b: (b, 0)),
            pl.BlockSpec((None, 8, seq), lambda b: (b, 0, 0)),
            pl.BlockSpec((1, LANES), lambda b: (0, 0)),
        ],
        out_specs=pl.BlockSpec((seq, DN_WIDTH), lambda b: (b, 0)),
        out_shape=jax.ShapeDtypeStruct((t, DN_WIDTH), BF16),
        scratch_shapes=[pltpu.VMEM((DN_HEADS, DN_HEAD_DIM, DN_HEAD_DIM), F32)],
        compiler_params=pltpu.CompilerParams(
            dimension_semantics=("arbitrary",), vmem_limit_bytes=VMEM_LIMIT),
        name="gdn",
    )(proj, proj, ab, abt, norm_g)


def _merge_kernel(x_ref, gates_ref, gm_ref, yb_ref, yc_ref, p_ref, ws_ref, bst_ref,
                  wa_ref, wb_ref, wc_ref, wo_ref, png_ref, wpg_ref, wpp_ref, fng_ref,
                  o_ref, ya_ref, *, rc, final):
    tm = x_ref.shape[0]

    ri = lax.broadcasted_iota(jnp.int32, (GM_CHUNK, GM_CHUNK), 0)
    ci = lax.broadcasted_iota(jnp.int32, (GM_CHUNK, GM_CHUNK), 1)
    wtri = [jnp.where(ri >= ci, ws_ref[g], 0.0).astype(BF16) for g in range(GM_GROUPS)]

    def gm_body(c, carry):
        r = pl.ds(pl.multiple_of(c * GM_CHUNK, GM_CHUNK), GM_CHUNK)
        for g in range(GM_GROUPS):
            cols = slice(g * LANES, (g + 1) * LANES)
            u = gm_ref[r, cols].astype(F32)
            v = gm_ref[r, GM_WIDTH + g * LANES:GM_WIDTH + (g + 1) * LANES]
            z = gm_ref[r, 2 * GM_WIDTH + g * LANES:2 * GM_WIDTH + (g + 1) * LANES].astype(F32)
            mix = jnp.dot(wtri[g], v, preferred_element_type=F32) + bst_ref[:, g:g + 1]
            ya_ref[r, cols] = (u * mix * z).astype(BF16)
        return carry
    lax.fori_loop(0, tm // GM_CHUNK, gm_body, 0)

    def body(c, carry):
        r = pl.ds(pl.multiple_of(c * rc, rc), rc)
        ga = gates_ref[r, 0:D_MODEL].astype(F32)
        gb = gates_ref[r, D_MODEL:2 * D_MODEL].astype(F32)
        gc = gates_ref[r, 2 * D_MODEL:3 * D_MODEL].astype(F32)
        m = (ga * jnp.dot(ya_ref[r, :], wa_ref[...], preferred_element_type=F32)
             + gb * jnp.dot(yb_ref[r, :], wb_ref[...], preferred_element_type=F32)
             + gc * jnp.dot(yc_ref[r, :], wc_ref[...], preferred_element_type=F32))
        x1 = x_ref[r, :] + jnp.dot(m.astype(BF16), wo_ref[...], preferred_element_type=F32)
        hn = x1 * lax.rsqrt(jnp.mean(x1 * x1, axis=-1, keepdims=True) + EPS) * png_ref[...]
        gate = _sigmoid(jnp.dot(hn.astype(BF16), wpg_ref[...], preferred_element_type=F32))
        pp = jnp.dot(p_ref[r, :].astype(BF16), wpp_ref[...], preferred_element_type=F32)
        x2 = x1 + gate * pp
        if final:
            x2 = x2 * lax.rsqrt(jnp.mean(x2 * x2, axis=-1, keepdims=True) + EPS) * fng_ref[...]
        o_ref[r, :] = x2
        return carry
    lax.fori_loop(0, tm // rc, body, 0)


def _merge(x2, proj, yb, yc, p2, ws, bst, wa, wb, wc, wo, png, wpg, wpp, fng, *, final):
    t = x2.shape[0]
    tm = 512
    rc = 256
    rows = lambda i: (i, 0)
    const2 = lambda i: (0, 0)
    return pl.pallas_call(
        functools.partial(_merge_kernel, rc=rc, final=final),
        grid=(t // tm,),
        in_specs=[
            pl.BlockSpec((tm, D_MODEL), rows),
            pl.BlockSpec((tm, 3 * D_MODEL), lambda i: (i, CB_GATES)),
            pl.BlockSpec((tm, 3 * GM_WIDTH), lambda i: (i, CB_GM_U // 3)),
            pl.BlockSpec((tm, DA_HEADS * LANES), rows),
            pl.BlockSpec((tm, DN_WIDTH), rows),
            pl.BlockSpec((tm, PLE_DIM), rows),
            pl.BlockSpec((GM_GROUPS, GM_CHUNK, GM_CHUNK), lambda i: (0, 0, 0)),
            pl.BlockSpec((GM_CHUNK, GM_GROUPS), const2),
            pl.BlockSpec((GM_WIDTH, D_MODEL), const2),
            pl.BlockSpec((DA_HEADS * LANES, D_MODEL), const2),
            pl.BlockSpec((DN_WIDTH, D_MODEL), const2),
            pl.BlockSpec((D_MODEL, D_MODEL), const2),
            pl.BlockSpec((1, D_MODEL), const2),
            pl.BlockSpec((D_MODEL, D_MODEL), const2),
            pl.BlockSpec((PLE_DIM, D_MODEL), const2),
            pl.BlockSpec((1, D_MODEL), const2),
        ],
        out_specs=pl.BlockSpec((tm, D_MODEL), rows),
        out_shape=jax.ShapeDtypeStruct((t, D_MODEL), F32),
        scratch_shapes=[pltpu.VMEM((tm, GM_WIDTH), BF16)],
        compiler_params=pltpu.CompilerParams(
            dimension_semantics=("arbitrary",), vmem_limit_bytes=VMEM_LIMIT),
        name="merge",
    )(x2, proj, proj, yb, yc, p2, ws, bst, wa, wb, wc, wo, png, wpg, wpp, fng)


def _permute_w_in(w):
    o_gm, o_da, o_dn = 0, 3 * GM_WIDTH, 3 * GM_WIDTH + 4 * 512
    o_ab = o_dn + 3 * DN_WIDTH
    o_dnz = o_ab + 2 * DN_HEADS
    o_gates = o_dnz + DN_WIDTH
    sl = lambda a, n: w[..., a:a + n]
    main = jnp.concatenate([
        sl(o_gates, 3 * D_MODEL),
        sl(o_gm, 3 * GM_WIDTH),
        sl(o_dn, 3 * DN_WIDTH),
        sl(o_da, 4 * 512),
        sl(o_dnz, DN_WIDTH),
    ], axis=-1).astype(BF16)
    ab = jnp.pad(sl(o_ab, 2 * DN_HEADS), ((0, 0), (0, 0), (0, LANES - 2 * DN_HEADS))).astype(BF16)
    return main, ab


def _rotary_tables(positions):
    inv_freq = ROPE_THETA ** (-jnp.arange(0, ROPE_DIM, 2, dtype=F32) / ROPE_DIM)
    ang = positions.astype(F32)[..., None] * inv_freq
    cos, sin = jnp.cos(ang), jnp.sin(ang)
    rest = DA_HEAD_DIM - ROPE_DIM
    tc = jnp.concatenate([cos, cos, jnp.ones(ang.shape[:-1] + (rest,), F32)], axis=-1)
    ts = jnp.concatenate([-sin, sin, jnp.zeros(ang.shape[:-1] + (rest,), F32)], axis=-1)
    rep = LANES // DA_HEAD_DIM
    tc = jnp.tile(tc, (1, 1, rep)).reshape(-1, LANES)
    ts = jnp.tile(ts, (1, 1, rep)).reshape(-1, LANES)
    return tc, ts


def _row(v, width=None):
    v = v.reshape(1, -1).astype(F32)
    if width is not None and v.shape[1] < width:
        v = jnp.pad(v, ((0, 0), (0, width - v.shape[1])))
    return v


def kernel(x, p, positions, norm_g, w_in, gm_ln_g, gm_ln_b, gm_ws, gm_bs, da_lq1, da_lk1, da_lq2, da_lk2, da_subln_g, dn_conv_w, dn_a_log, dn_dt_bias, dn_norm_g, w_br_a, w_br_b, w_br_c, w_out, ple_norm_g, w_ple_gate, w_ple_proj, final_norm_g):
    batch, seq, _ = x.shape
    depth = w_in.shape[0]
    t = batch * seq
    assert seq % DN_CHUNK_ROWS == 0 and seq % INPROJ_ROWS == 0

    w_main, w_ab = _permute_w_in(w_in)
    tabc, tabs = _rotary_tables(positions)
    wa16, wb16, wc16 = w_br_a.astype(BF16), w_br_b.astype(BF16), w_br_c.astype(BF16)
    wo16, wpg16, wpp16 = w_out.astype(BF16), w_ple_gate.astype(BF16), w_ple_proj.astype(BF16)
    fng = _row(final_norm_g)

    xc = x.reshape(t, D_MODEL)
    for i in range(depth):
        lambda_init = 0.8 - 0.6 * math.exp(-0.3 * i)
        proj, ab = _inproj(
            xc, _row(norm_g[i]), w_main[i], w_ab[i], _row(gm_ln_g[i]), _row(gm_ln_b[i]), tabc, tabs,
            _row(dn_a_log[i], LANES), _row(dn_dt_bias[i], LANES), dn_conv_w[i].astype(F32), seq=seq)
        lam_p = jnp.pad(jnp.stack([da_lq1[i], da_lk1[i], da_lq2[i], da_lk2[i]]).astype(F32),
                        ((0, 4), (0, LANES - DA_HEAD_DIM)))
        yb = _attn(proj, lam_p, _row(da_subln_g[i]), batch=batch, seq=seq, lambda_init=lambda_init)
        abt = jnp.swapaxes(ab.reshape(batch, seq, LANES)[:, :, :8], 1, 2)
        yc = _gdn(proj, ab, abt, _row(dn_norm_g[i]), batch=batch, seq=seq)
        xc = _merge(xc, proj, yb, yc, p[i].reshape(t, PLE_DIM), gm_ws[i].astype(F32),
                    gm_bs[i].T.astype(F32), wa16[i], wb16[i], wc16[i], wo16[i],
                    _row(ple_norm_g[i]), wpg16[i], wpp16[i], fng, final=(i == depth - 1))
    return xc.reshape(batch, seq, D_MODEL)
```

```python
import functools
import math

import jax
import jax.numpy as jnp
from jax import lax
from jax.experimental import pallas as pl
from jax.experimental.pallas import tpu as pltpu

F32 = jnp.float32
BF16 = jnp.bfloat16

D_MODEL = 1024
PLE_DIM = 256
EPS = 1e-6
GM_GROUPS = 4
GM_CHUNK = 128
GM_WIDTH = 512
DA_HEADS = 4
DA_HEAD_DIM = 64
ROPE_THETA = 500000.0
ROPE_DIM = 16
DN_HEADS = 4
DN_HEAD_DIM = 128
DN_WIDTH = 512
DN_CONV = 4

LANES = 128
BF16_SUBLANES = 16
COL_BLOCK = 512
N_COL_BLOCKS = 17
CB_GATES = 0
CB_GM_U, CB_GM_V, CB_GM_Z = 6, 7, 8
CB_DN_QKV = 9
CB_DA_Q, CB_DA_K, CB_DA_V, CB_DA_Z = 12, 13, 14, 15
CB_DN_Z = 16
W_IN_AB_START = 3 * GM_WIDTH + 4 * 512 + 3 * DN_WIDTH
W_LO_BLOCK = {CB_GM_U: 0, CB_GM_V: 1, CB_GM_Z: 2, CB_DA_Q: 3, CB_DA_K: 4, CB_DA_V: 5, CB_DA_Z: 6,
              CB_DN_QKV: 7, CB_DN_QKV + 1: 8, CB_DN_QKV + 2: 9}
W_HI_BLOCK = {CB_DN_Z: 0, **{CB_GATES + g: 1 + g for g in range(6)}}
VMEM_LIMIT = 56 * 1024 * 1024
DN_CHUNK_ROWS = 256
DN_GROUP_UNROLL = 4
CONV_PAD = 8
LOG2_E = math.log2(math.e)
INPROJ_ROWS = 512
INPROJ_ORDER = (9, 0, 10, 1, 11, 2, 7, 3, 6, 4, 12, 5, 13, 14, 8, 15, 16)


def _sigmoid(x):
    return 0.5 * jnp.tanh(0.5 * x) + 0.5


def _gelu_tanh(x):
    c = math.sqrt(2.0 / math.pi)
    return 0.5 * x * (1.0 + jnp.tanh(c * (x + 0.044715 * (x * x * x))))


def _nt_dot(a, b):
    return lax.dot_general(a, b, (((1,), (1,)), ((), ())), preferred_element_type=F32)


def _inproj_kernel(x_ref, ng_ref, wlo_ref, whi_ref, wab_ref, lng_ref, lnb_ref, tc_ref, ts_ref,
                   alog_ref, dtb_ref, cw_ref, proj_ref, ab_ref, h_ref, tail_ref, *, tiles_per_seq):
    tm = x_ref.shape[0]

    xv = x_ref[...]
    ms = jnp.mean(xv * xv, axis=-1, keepdims=True)
    hb = (xv * lax.rsqrt(ms + EPS) * ng_ref[...]).astype(BF16)
    h_ref[...] = hb
    z = jnp.dot(hb, wab_ref[...], preferred_element_type=F32)
    lane = lax.broadcasted_iota(jnp.int32, z.shape, 1)
    zz = z + dtb_ref[...]
    sp = jnp.maximum(zz, 0.0) + jnp.log1p(jnp.exp(-jnp.abs(zz)))
    ab_ref[...] = jnp.where(lane < DN_HEADS, -jnp.exp(alog_ref[...]) * sp, _sigmoid(z))

    @pl.when(pl.program_id(0) % tiles_per_seq == 0)
    def _():
        tail_ref[...] = jnp.zeros(tail_ref.shape, F32)

    def ep_sig(j, acc):
        return _sigmoid(acc)

    def ep_gelu(j, acc):
        return _gelu_tanh(acc)

    def ep_geln(j, acc):
        g = _gelu_tanh(acc)
        mu = jnp.mean(g, axis=-1, keepdims=True)
        gc = g - mu
        var = jnp.mean(gc * gc, axis=-1, keepdims=True)
        return gc * lax.rsqrt(var + EPS) * lng_ref[...] + lnb_ref[...]

    def ep_silu(j, acc):
        return acc * _sigmoid(acc)

    def ep_id(j, acc):
        return acc

    def ep_rot(j, acc):
        scale = DA_HEAD_DIM ** -0.5 * LOG2_E if j == CB_DA_Q else 1.0
        tcv = tc_ref[...] * scale
        tsv = ts_ref[...] * scale
        lane = lax.broadcasted_iota(jnp.int32, tcv.shape, 1)
        first_half = (lane & (DA_HEAD_DIM - 1)) < (ROPE_DIM // 2)
        outs = []
        for s in range(COL_BLOCK // LANES):
            a = acc[:, s * LANES:(s + 1) * LANES]
            up = pltpu.roll(a, LANES - ROPE_DIM // 2, 1)
            dn = pltpu.roll(a, ROPE_DIM // 2, 1)
            outs.append(a * tcv + jnp.where(first_half, up, dn) * tsv)
        return jnp.concatenate(outs, axis=1)

    def ep_dn(j, acc):
        part = j - CB_DN_QKV
        xa = jnp.concatenate([tail_ref[part], acc], axis=0)
        tail_ref[part] = acc[tm - CONV_PAD:, :]
        w = cw_ref[:, part * COL_BLOCK:(part + 1) * COL_BLOCK]
        y = acc * w[DN_CONV - 1:DN_CONV, :]
        for tap in range(DN_CONV - 1):
            lo = CONV_PAD - (DN_CONV - 1 - tap)
            y = y + xa[lo:lo + tm, :] * w[tap:tap + 1, :]
        y = y * _sigmoid(y)
        if part == 2:
            return y
        scale = DN_HEAD_DIM ** -0.5 if part == 0 else 1.0
        outs = []
        for s in range(COL_BLOCK // LANES):
            ys = y[:, s * LANES:(s + 1) * LANES]
            outs.append(ys * (lax.rsqrt(jnp.sum(ys * ys, axis=-1, keepdims=True) + EPS) * scale))
        return jnp.concatenate(outs, axis=1)

    kind = {CB_GM_U: ep_gelu, CB_GM_V: ep_geln, CB_GM_Z: ep_silu, CB_DA_Z: ep_silu, CB_DN_Z: ep_silu,
            CB_DA_V: ep_id, CB_DA_Q: ep_rot, CB_DA_K: ep_rot}
    kind.update({CB_GATES + g: ep_sig for g in range(6)})
    kind.update({CB_DN_QKV + g: ep_dn for g in range(3)})

    def project(j):
        src_ref, sb = (wlo_ref, W_LO_BLOCK[j]) if j in W_LO_BLOCK else (whi_ref, W_HI_BLOCK[j])
        wj = src_ref[:, sb * COL_BLOCK:(sb + 1) * COL_BLOCK]
        return jnp.dot(h_ref[...], wj, preferred_element_type=F32)

    acc = project(INPROJ_ORDER[0])
    for idx, j in enumerate(INPROJ_ORDER):
        nxt = project(INPROJ_ORDER[idx + 1]) if idx + 1 < N_COL_BLOCKS else None
        proj_ref[:, j * COL_BLOCK:(j + 1) * COL_BLOCK] = kind[j](j, acc).astype(BF16)
        acc = nxt


def _inproj(x2, ng, wlo, whi, wab, lng, lnb, tabc, tabs, alog, dtb, conv_w, *, seq, layer):
    t = x2.shape[0]
    tm = INPROJ_ROWS
    ncols = N_COL_BLOCKS * COL_BLOCK
    const = lambda i: (0, 0)
    rows = lambda i: (i, 0)
    lyr = lambda i: (layer, 0, 0)
    resident = pl.Buffered(1)
    return pl.pallas_call(
        functools.partial(_inproj_kernel, tiles_per_seq=seq // tm),
        grid=(t // tm,),
        in_specs=[
            pl.BlockSpec((tm, D_MODEL), rows),
            pl.BlockSpec((1, D_MODEL), const),
            pl.BlockSpec((None, D_MODEL, wlo.shape[2]), lyr, pipeline_mode=resident),
            pl.BlockSpec((None, D_MODEL, whi.shape[2]), lyr, pipeline_mode=resident),
            pl.BlockSpec((None, D_MODEL, LANES), lyr, pipeline_mode=resident),
            pl.BlockSpec((1, COL_BLOCK), const),
            pl.BlockSpec((1, COL_BLOCK), const),
            pl.BlockSpec((tm, LANES), rows),
            pl.BlockSpec((tm, LANES), rows),
            pl.BlockSpec((1, LANES), const),
            pl.BlockSpec((1, LANES), const),
            pl.BlockSpec((DN_CONV, 3 * DN_WIDTH), const),
        ],
        out_specs=[
            pl.BlockSpec((tm, ncols), rows),
            pl.BlockSpec((tm, LANES), rows),
        ],
        out_shape=[
            jax.ShapeDtypeStruct((t, ncols), BF16),
            jax.ShapeDtypeStruct((t, LANES), F32),
        ],
        scratch_shapes=[pltpu.VMEM((tm, D_MODEL), BF16),
                        pltpu.VMEM((3, CONV_PAD, COL_BLOCK), F32)],
        compiler_params=pltpu.CompilerParams(
            dimension_semantics=("arbitrary",), vmem_limit_bytes=VMEM_LIMIT),
        name="inproj",
    )(x2, ng, wlo, whi, wab, lng, lnb, tabc, tabs, alog, dtb, conv_w)


def _attn_kernel(q_ref, k_ref, v_ref, z_ref, lam_ref, sg_ref, o_ref, vt_ref, *, lambda_init, tq):
    seq = q_ref.shape[0]
    tk = tq
    nq = seq // tq

    lp = lam_ref[...]
    lam = (jnp.exp(jnp.sum(lp[0:1] * lp[1:2], axis=-1, keepdims=True))
           - jnp.exp(jnp.sum(lp[2:3] * lp[3:4], axis=-1, keepdims=True)) + lambda_init)

    for c0 in range(0, seq, tk):
        vt_ref[:, c0:c0 + tk] = v_ref[c0:c0 + tk, :].astype(F32).T.astype(BF16)

    def scores(qs, kb):
        ks = k_ref[kb * tk:(kb + 1) * tk, :]
        return [_nt_dot(ks, qs[c]) for c in range(2)]

    def softmax_pv(kb, sts, carry, masked):
        vts = vt_ref[:, kb * tk:(kb + 1) * tk]
        out = []
        for c in range(2):
            m, l, a = carry[c]
            st = sts[c]
            if masked:
                ri = lax.broadcasted_iota(jnp.int32, st.shape, 0)
                ci = lax.broadcasted_iota(jnp.int32, st.shape, 1)
                st = jnp.where(ri <= ci, st, -jnp.inf)
            mn = jnp.maximum(m, jnp.max(st, axis=0, keepdims=True))
            alpha = jnp.exp2(m - mn)
            pt = jnp.exp2(st - mn)
            l = alpha * l + jnp.sum(pt, axis=0, keepdims=True)
            a = alpha * a + jnp.dot(vts, pt.astype(BF16), preferred_element_type=F32)
            out.append((mn, l, a))
        return tuple(out)

    for qi in range(nq):
        qr = slice(qi * tq, (qi + 1) * tq)
        q = q_ref[qr, :]
        lane = lax.broadcasted_iota(jnp.int32, q.shape, 1)
        zero = jnp.zeros_like(q)
        qs = (jnp.where(lane < DA_HEAD_DIM, q, zero), jnp.where(lane >= DA_HEAD_DIM, q, zero))
        carry = tuple((jnp.full((1, tq), -1e30, F32), jnp.zeros((1, tq), F32),
                       jnp.zeros((LANES, tq), F32)) for _ in range(2))
        sts = scores(qs, 0)
        for kb in range(qi + 1):
            nxt = scores(qs, kb + 1) if kb < qi else None
            carry = softmax_pv(kb, sts, carry, kb == qi)
            sts = nxt
        (_, l0, a0), (_, l1, a1) = carry

        ot = a0 / l0 - lam * (a1 / l1)
        ot = ot * lax.rsqrt(jnp.mean(ot * ot, axis=0, keepdims=True) + EPS)
        o = ot.T * sg_ref[...] * (1.0 - lambda_init)
        o_ref[qr, :] = (o * z_ref[qr, :].astype(F32)).astype(BF16)


def _attn(proj, lam_p, subln_g, *, batch, seq, lambda_init):
    t = proj.shape[0]
    per = COL_BLOCK // LANES
    tq = min(seq, 512)
    col = lambda cb: (lambda b, h: (b, cb * per + h))
    return pl.pallas_call(
        functools.partial(_attn_kernel, lambda_init=lambda_init, tq=tq),
        grid=(batch, DA_HEADS),
        in_specs=[
            pl.BlockSpec((seq, LANES), col(CB_DA_Q)),
            pl.BlockSpec((seq, LANES), col(CB_DA_K)),
            pl.BlockSpec((seq, LANES), col(CB_DA_V)),
            pl.BlockSpec((seq, LANES), col(CB_DA_Z)),
            pl.BlockSpec((8, LANES), lambda b, h: (0, 0)),
            pl.BlockSpec((1, LANES), lambda b, h: (0, 0)),
        ],
        out_specs=pl.BlockSpec((seq, LANES), lambda b, h: (b, h)),
        out_shape=jax.ShapeDtypeStruct((t, DA_HEADS * LANES), BF16),
        scratch_shapes=[pltpu.VMEM((LANES, seq), BF16)],
        compiler_params=pltpu.CompilerParams(
            dimension_semantics=("arbitrary", "arbitrary"), vmem_limit_bytes=VMEM_LIMIT),
        name="attn",
    )(proj, proj, proj, proj, lam_p, subln_g)


def _segment_cumsum(x, axis, seg):
    idx = lax.broadcasted_iota(jnp.int32, x.shape, axis) & (seg - 1)
    s = 1
    while s < seg:
        x = x + jnp.where(idx >= s, pltpu.roll(x, s, axis), 0.0)
        s *= 2
    return x


def _gdn_kernel(qkv_ref, z_ref, ab_ref, abt_ref, ng_ref, o_ref, st_ref):
    seq = qkv_ref.shape[0]
    cr = DN_CHUNK_ROWS
    ngroups = seq // cr
    nh = DN_HEADS

    st_ref[...] = jnp.zeros(st_ref.shape, F32)

    def load(r, part, hh):
        col = part * DN_WIDTH + hh * LANES
        return qkv_ref[r, col:col + LANES]

    def group(g, carry):
        r = pl.ds(pl.multiple_of(g * cr, cr), cr)
        abv = ab_ref[r, :]
        gcol = _segment_cumsum(abv, 0, cr)
        grow = _segment_cumsum(abt_ref[:, r], 1, cr)
        ri = lax.broadcasted_iota(jnp.int32, (cr, cr), 0)
        ci = lax.broadcasted_iota(jnp.int32, (cr, cr), 1)
        causal = ri >= ci
        strict = ri > ci
        heads = range(nh)
        dot = functools.partial(jnp.dot, preferred_element_type=F32)
        gcc = [gcol[:, hh:hh + 1] for hh in heads]
        bcol = [abv[:, nh + hh:nh + hh + 1] for hh in heads]
        decay = [jnp.exp(jnp.where(causal, gcc[hh] - grow[hh:hh + 1, :], -jnp.inf)) for hh in heads]
        q16 = [load(r, 0, hh) for hh in heads]
        k16 = [load(r, 1, hh) for hh in heads]
        qf = [q16[hh].astype(F32) for hh in heads]
        kf = [k16[hh].astype(F32) for hh in heads]
        vf = [load(r, 2, hh).astype(F32) for hh in heads]
        a = [jnp.where(strict, (bcol[hh] * _nt_dot(k16[hh], k16[hh])) * decay[hh], 0.0) for hh in heads]
        n = [-a[hh] for hh in heads]
        pw16 = [a[hh].astype(BF16) for hh in heads]
        for level in range(1, int(math.log2(cr))):
            lo = 2 ** level if 2 ** level >= BF16_SUBLANES else 0
            pw = [dot(pw16[hh][lo:, :], pw16[hh]) for hh in heads]
            pw16 = [pw[hh].astype(BF16) for hh in heads]
            if lo:
                pw16 = [jnp.concatenate([jnp.zeros((lo, cr), BF16), pw16[hh]], axis=0) for hh in heads]
            nlo = [n[hh][lo:, :] for hh in heads]
            nlo = [nlo[hh] + pw[hh] + dot(nlo[hh].astype(BF16), pw16[hh]) for hh in heads]
            n = [jnp.concatenate([n[hh][:lo, :], nlo[hh]], axis=0) if lo else nlo[hh] for hh in heads]
        eg = [jnp.exp(gcc[hh]) for hh in heads]
        rhs = [jnp.concatenate([vf[hh] * bcol[hh], kf[hh] * (bcol[hh] * eg[hh])], axis=1) for hh in heads]
        uw = [rhs[hh] + dot(n[hh].astype(BF16), rhs[hh].astype(BF16)) for hh in heads]
        qk = [_nt_dot(q16[hh], k16[hh]) * decay[hh] for hh in heads]
        gl = [gcc[hh][cr - 1:cr, :] for hh in heads]
        kdt = [(kf[hh] * jnp.exp(gl[hh] - gcc[hh])).T.astype(BF16) for hh in heads]
        st = [st_ref[hh] for hh in heads]
        st16 = [st[hh].astype(BF16) for hh in heads]
        v16 = [(uw[hh][:, :LANES] - dot(uw[hh][:, LANES:].astype(BF16), st16[hh])).astype(BF16) for hh in heads]
        o = [dot((qf[hh] * eg[hh]).astype(BF16), st16[hh]) + dot(qk[hh].astype(BF16), v16[hh]) for hh in heads]
        for hh in heads:
            st_ref[hh] = st[hh] * jnp.exp(gl[hh]) + dot(kdt[hh], v16[hh])
            oh = o[hh] * lax.rsqrt(jnp.mean(o[hh] * o[hh], axis=-1, keepdims=True) + EPS) * ng_ref[...]
            zc = z_ref[r, hh * LANES:(hh + 1) * LANES].astype(F32)
            o_ref[r, hh * LANES:(hh + 1) * LANES] = (oh * zc).astype(BF16)
        return carry

    lax.fori_loop(0, ngroups, group, 0, unroll=DN_GROUP_UNROLL)


def _gdn(proj, ab, abt, norm_g, *, batch, seq):
    t = proj.shape[0]
    return pl.pallas_call(
        _gdn_kernel,
        grid=(batch,),
        in_specs=[
            pl.BlockSpec((seq, 3 * DN_WIDTH), lambda b: (b, CB_DN_QKV // 3)),
            pl.BlockSpec((seq, DN_WIDTH), lambda b: (b, CB_DN_Z)),
            pl.BlockSpec((seq, LANES), lambda b: (b, 0)),
            pl.BlockSpec((None, 8, seq), lambda b: (b, 0, 0)),
            pl.BlockSpec((1, LANES), lambda b: (0, 0)),
        ],
        out_specs=pl.BlockSpec((seq, DN_WIDTH), lambda b: (b, 0)),
        out_shape=jax.ShapeDtypeStruct((t, DN_WIDTH), BF16),
        scratch_shapes=[pltpu.VMEM((DN_HEADS, DN_HEAD_DIM, DN_HEAD_DIM), F32)],
        compiler_params=pltpu.CompilerParams(
            dimension_semantics=("arbitrary",), vmem_limit_bytes=VMEM_LIMIT),
        name="gdn",
    )(proj, proj, ab, abt, norm_g)


def _merge_kernel(x_ref, gates_ref, gm_ref, yb_ref, yc_ref, p_ref, ws_ref, bst_ref,
                  wa_ref, wb_ref, wc_ref, wo_ref, png_ref, wpg_ref, wpp_ref, fng_ref,
                  o_ref, ya_ref, *, rc, final):
    tm = x_ref.shape[0]

    ri = lax.broadcasted_iota(jnp.int32, (GM_CHUNK, GM_CHUNK), 0)
    ci = lax.broadcasted_iota(jnp.int32, (GM_CHUNK, GM_CHUNK), 1)
    wtri = [jnp.where(ri >= ci, ws_ref[g], 0.0).astype(BF16) for g in range(GM_GROUPS)]

    for c in range(tm // GM_CHUNK):
        r = slice(c * GM_CHUNK, (c + 1) * GM_CHUNK)
        for g in range(GM_GROUPS):
            cols = slice(g * LANES, (g + 1) * LANES)
            u = gm_ref[r, cols].astype(F32)
            v = gm_ref[r, GM_WIDTH + g * LANES:GM_WIDTH + (g + 1) * LANES]
            z = gm_ref[r, 2 * GM_WIDTH + g * LANES:2 * GM_WIDTH + (g + 1) * LANES].astype(F32)
            mix = jnp.dot(wtri[g], v, preferred_element_type=F32) + bst_ref[:, g:g + 1]
            ya_ref[r, cols] = (u * mix * z).astype(BF16)

    for c in range(tm // rc):
        r = slice(c * rc, (c + 1) * rc)
        ga = gates_ref[r, 0:D_MODEL].astype(F32)
        gb = gates_ref[r, D_MODEL:2 * D_MODEL].astype(F32)
        gc = gates_ref[r, 2 * D_MODEL:3 * D_MODEL].astype(F32)
        m = (ga * jnp.dot(ya_ref[r, :], wa_ref[...], preferred_element_type=F32)
             + gb * jnp.dot(yb_ref[r, :], wb_ref[...], preferred_element_type=F32)
             + gc * jnp.dot(yc_ref[r, :], wc_ref[...], preferred_element_type=F32))
        x1 = x_ref[r, :] + jnp.dot(m.astype(BF16), wo_ref[...], preferred_element_type=F32)
        hn = x1 * lax.rsqrt(jnp.mean(x1 * x1, axis=-1, keepdims=True) + EPS) * png_ref[...]
        gate = _sigmoid(jnp.dot(hn.astype(BF16), wpg_ref[...], preferred_element_type=F32))
        pp = jnp.dot(p_ref[r, :].astype(BF16), wpp_ref[...], preferred_element_type=F32)
        x2 = x1 + gate * pp
        if final:
            x2 = x2 * lax.rsqrt(jnp.mean(x2 * x2, axis=-1, keepdims=True) + EPS) * fng_ref[...]
        o_ref[r, :] = x2


def _merge(x2, proj, yb, yc, p3, ws, bst, wa, wb, wc, wo, png, wpg, wpp, fng, *, final, layer):
    t = x2.shape[0]
    tm = 512
    rc = 256
    rows = lambda i: (i, 0)
    const2 = lambda i: (0, 0)
    lyr = lambda i: (layer, 0, 0)
    resident = pl.Buffered(1)
    wspec = lambda k: pl.BlockSpec((None, k, D_MODEL), lyr, pipeline_mode=resident)
    return pl.pallas_call(
        functools.partial(_merge_kernel, rc=rc, final=final),
        grid=(t // tm,),
        in_specs=[
            pl.BlockSpec((tm, D_MODEL), rows),
            pl.BlockSpec((tm, 3 * D_MODEL), lambda i: (i, CB_GATES)),
            pl.BlockSpec((tm, 3 * GM_WIDTH), lambda i: (i, CB_GM_U // 3)),
            pl.BlockSpec((tm, DA_HEADS * LANES), rows),
            pl.BlockSpec((tm, DN_WIDTH), rows),
            pl.BlockSpec((None, tm, PLE_DIM), lambda i: (layer, i, 0)),
            pl.BlockSpec((None, GM_GROUPS, GM_CHUNK, GM_CHUNK), lambda i: (layer, 0, 0, 0)),
            pl.BlockSpec((GM_CHUNK, GM_GROUPS), const2),
            wspec(GM_WIDTH),
            wspec(DA_HEADS * LANES),
            wspec(DN_WIDTH),
            wspec(D_MODEL),
            pl.BlockSpec((1, D_MODEL), const2),
            wspec(D_MODEL),
            wspec(PLE_DIM),
            pl.BlockSpec((1, D_MODEL), const2),
        ],
        out_specs=pl.BlockSpec((tm, D_MODEL), rows),
        out_shape=jax.ShapeDtypeStruct((t, D_MODEL), F32),
        scratch_shapes=[pltpu.VMEM((tm, GM_WIDTH), BF16)],
        compiler_params=pltpu.CompilerParams(
            dimension_semantics=("arbitrary",), vmem_limit_bytes=VMEM_LIMIT),
        name="merge",
    )(x2, proj, proj, yb, yc, p3, ws, bst, wa, wb, wc, wo, png, wpg, wpp, fng)


def _split_w_in(w):
    lo = w[..., :W_IN_AB_START].astype(BF16)
    hi = w[..., W_IN_AB_START + 2 * DN_HEADS:].astype(BF16)
    ab = jnp.pad(w[..., W_IN_AB_START:W_IN_AB_START + 2 * DN_HEADS],
                 ((0, 0), (0, 0), (0, LANES - 2 * DN_HEADS))).astype(BF16)
    return lo, hi, ab


def _rotary_tables(positions):
    inv_freq = ROPE_THETA ** (-jnp.arange(0, ROPE_DIM, 2, dtype=F32) / ROPE_DIM)
    ang = positions.astype(F32)[..., None] * inv_freq
    cos, sin = jnp.cos(ang), jnp.sin(ang)
    rest = DA_HEAD_DIM - ROPE_DIM
    tc = jnp.concatenate([cos, cos, jnp.ones(ang.shape[:-1] + (rest,), F32)], axis=-1)
    ts = jnp.concatenate([-sin, sin, jnp.zeros(ang.shape[:-1] + (rest,), F32)], axis=-1)
    rep = LANES // DA_HEAD_DIM
    tc = jnp.tile(tc, (1, 1, rep)).reshape(-1, LANES)
    ts = jnp.tile(ts, (1, 1, rep)).reshape(-1, LANES)
    return tc, ts


def _row(v, width=None):
    v = v.reshape(1, -1).astype(F32)
    if width is not None and v.shape[1] < width:
        v = jnp.pad(v, ((0, 0), (0, width - v.shape[1])))
    return v


def kernel(x, p, positions, norm_g, w_in, gm_ln_g, gm_ln_b, gm_ws, gm_bs, da_lq1, da_lk1, da_lq2, da_lk2, da_subln_g, dn_conv_w, dn_a_log, dn_dt_bias, dn_norm_g, w_br_a, w_br_b, w_br_c, w_out, ple_norm_g, w_ple_gate, w_ple_proj, final_norm_g):
    batch, seq, _ = x.shape
    depth = w_in.shape[0]
    t = batch * seq
    assert seq % DN_CHUNK_ROWS == 0 and seq % INPROJ_ROWS == 0

    w_lo, w_hi, w_ab = _split_w_in(w_in)
    tabc, tabs = _rotary_tables(positions)
    wa16, wb16, wc16 = w_br_a.astype(BF16), w_br_b.astype(BF16), w_br_c.astype(BF16)
    wo16, wpg16, wpp16 = w_out.astype(BF16), w_ple_gate.astype(BF16), w_ple_proj.astype(BF16)
    fng = _row(final_norm_g)
    p3 = p.reshape(depth, t, PLE_DIM)
    gm_ws = gm_ws.astype(F32)

    xc = x.reshape(t, D_MODEL)
    for i in range(depth):
        lambda_init = 0.8 - 0.6 * math.exp(-0.3 * i)
        proj, ab = _inproj(
            xc, _row(norm_g[i]), w_lo, w_hi, w_ab, _row(gm_ln_g[i]), _row(gm_ln_b[i]), tabc, tabs,
            _row(dn_a_log[i], LANES), _row(dn_dt_bias[i], LANES), dn_conv_w[i].astype(F32),
            seq=seq, layer=i)
        lam_p = jnp.pad(jnp.stack([da_lq1[i], da_lk1[i], da_lq2[i], da_lk2[i]]).astype(F32),
                        ((0, 4), (0, LANES - DA_HEAD_DIM)))
        yb = _attn(proj, lam_p, _row(da_subln_g[i]), batch=batch, seq=seq, lambda_init=lambda_init)
        abt = jnp.swapaxes(ab.reshape(batch, seq, LANES)[:, :, :8], 1, 2)
        yc = _gdn(proj, ab, abt, _row(dn_norm_g[i]), batch=batch, seq=seq)
        xc = _merge(xc, proj, yb, yc, p3, gm_ws, gm_bs[i].T.astype(F32), wa16, wb16, wc16, wo16,
                    _row(ple_norm_g[i]), wpg16, wpp16, fng, final=(i == depth - 1), layer=i)
    return xc.reshape(batch, seq, D_MODEL)
```

```python
import functools
import math

import jax
import jax.numpy as jnp
from jax import lax
from jax.experimental import pallas as pl
from jax.experimental.pallas import tpu as pltpu

F32 = jnp.float32
BF16 = jnp.bfloat16

D_MODEL = 1024
PLE_DIM = 256
EPS = 1e-6
GM_GROUPS = 4
GM_CHUNK = 128
GM_WIDTH = 512
DA_HEADS = 4
DA_HEAD_DIM = 64
ROPE_THETA = 500000.0
ROPE_DIM = 16
DN_HEADS = 4
DN_HEAD_DIM = 128
DN_WIDTH = 512
DN_CONV = 4

LANES = 128
BF16_SUBLANES = 16
COL_BLOCK = 512
N_COL_BLOCKS = 17
CB_GATES = 0
CB_GM_U, CB_GM_V, CB_GM_Z = 6, 7, 8
CB_DN_QKV = 9
CB_DA_Q, CB_DA_K, CB_DA_V, CB_DA_Z = 12, 13, 14, 15
CB_DN_Z = 16
W_IN_AB_START = 3 * GM_WIDTH + 4 * 512 + 3 * DN_WIDTH
W_LO_BLOCK = {CB_GM_U: 0, CB_GM_V: 1, CB_GM_Z: 2, CB_DA_Q: 3, CB_DA_K: 4, CB_DA_V: 5, CB_DA_Z: 6,
              CB_DN_QKV: 7, CB_DN_QKV + 1: 8, CB_DN_QKV + 2: 9}
W_HI_BLOCK = {CB_DN_Z: 0, **{CB_GATES + g: 1 + g for g in range(6)}}
VMEM_LIMIT = 56 * 1024 * 1024
DN_CHUNK_ROWS = 256
DN_GROUP_UNROLL = 4
CONV_PAD = 8
LOG2_E = math.log2(math.e)
INPROJ_ROWS = 512
INPROJ_ORDER = (9, 0, 10, 1, 11, 2, 7, 3, 6, 4, 12, 5, 13, 14, 8, 15, 16)


def _sigmoid(x):
    return 0.5 * jnp.tanh(0.5 * x) + 0.5


def _gelu_tanh(x):
    c = math.sqrt(2.0 / math.pi)
    return 0.5 * x * (1.0 + jnp.tanh(c * (x + 0.044715 * (x * x * x))))


def _nt_dot(a, b):
    return lax.dot_general(a, b, (((1,), (1,)), ((), ())), preferred_element_type=F32)


def _inproj_kernel(x_ref, ng_ref, wlo_ref, whi_ref, wab_ref, lng_ref, lnb_ref, tc_ref, ts_ref,
                   alog_ref, dtb_ref, cw_ref, proj_ref, ab_ref, h_ref, tail_ref, *, tiles_per_seq):
    tm = x_ref.shape[0]

    xv = x_ref[...]
    ms = jnp.mean(xv * xv, axis=-1, keepdims=True)
    hb = (xv * lax.rsqrt(ms + EPS) * ng_ref[...]).astype(BF16)
    h_ref[...] = hb
    z = jnp.dot(hb, wab_ref[...], preferred_element_type=F32)
    lane = lax.broadcasted_iota(jnp.int32, z.shape, 1)
    zz = z + dtb_ref[...]
    sp = jnp.maximum(zz, 0.0) + jnp.log1p(jnp.exp(-jnp.abs(zz)))
    ab_ref[...] = jnp.where(lane < DN_HEADS, -jnp.exp(alog_ref[...]) * sp, _sigmoid(z))

    @pl.when(pl.program_id(0) % tiles_per_seq == 0)
    def _():
        tail_ref[...] = jnp.zeros(tail_ref.shape, F32)

    def ep_sig(j, acc):
        return _sigmoid(acc)

    def ep_gelu(j, acc):
        return _gelu_tanh(acc)

    def ep_geln(j, acc):
        g = _gelu_tanh(acc)
        mu = jnp.mean(g, axis=-1, keepdims=True)
        gc = g - mu
        var = jnp.mean(gc * gc, axis=-1, keepdims=True)
        return gc * lax.rsqrt(var + EPS) * lng_ref[...] + lnb_ref[...]

    def ep_silu(j, acc):
        return acc * _sigmoid(acc)

    def ep_id(j, acc):
        return acc

    def ep_rot(j, acc):
        scale = DA_HEAD_DIM ** -0.5 * LOG2_E if j == CB_DA_Q else 1.0
        tcv = tc_ref[...] * scale
        tsv = ts_ref[...] * scale
        lane = lax.broadcasted_iota(jnp.int32, tcv.shape, 1)
        first_half = (lane & (DA_HEAD_DIM - 1)) < (ROPE_DIM // 2)
        outs = []
        for s in range(COL_BLOCK // LANES):
            a = acc[:, s * LANES:(s + 1) * LANES]
            up = pltpu.roll(a, LANES - ROPE_DIM // 2, 1)
            dn = pltpu.roll(a, ROPE_DIM // 2, 1)
            outs.append(a * tcv + jnp.where(first_half, up, dn) * tsv)
        return jnp.concatenate(outs, axis=1)

    def ep_dn(j, acc):
        part = j - CB_DN_QKV
        xa = jnp.concatenate([tail_ref[part], acc], axis=0)
        tail_ref[part] = acc[tm - CONV_PAD:, :]
        w = cw_ref[:, part * COL_BLOCK:(part + 1) * COL_BLOCK]
        y = acc * w[DN_CONV - 1:DN_CONV, :]
        for tap in range(DN_CONV - 1):
            lo = CONV_PAD - (DN_CONV - 1 - tap)
            y = y + xa[lo:lo + tm, :] * w[tap:tap + 1, :]
        y = y * _sigmoid(y)
        if part == 2:
            return y
        scale = DN_HEAD_DIM ** -0.5 if part == 0 else 1.0
        outs = []
        for s in range(COL_BLOCK // LANES):
            ys = y[:, s * LANES:(s + 1) * LANES]
            outs.append(ys * (lax.rsqrt(jnp.sum(ys * ys, axis=-1, keepdims=True) + EPS) * scale))
        return jnp.concatenate(outs, axis=1)

    kind = {CB_GM_U: ep_gelu, CB_GM_V: ep_geln, CB_GM_Z: ep_silu, CB_DA_Z: ep_silu, CB_DN_Z: ep_silu,
            CB_DA_V: ep_id, CB_DA_Q: ep_rot, CB_DA_K: ep_rot}
    kind.update({CB_GATES + g: ep_sig for g in range(6)})
    kind.update({CB_DN_QKV + g: ep_dn for g in range(3)})

    def project(j):
        src_ref, sb = (wlo_ref, W_LO_BLOCK[j]) if j in W_LO_BLOCK else (whi_ref, W_HI_BLOCK[j])
        wj = src_ref[:, sb * COL_BLOCK:(sb + 1) * COL_BLOCK]
        return jnp.dot(h_ref[...], wj, preferred_element_type=F32)

    for j in INPROJ_ORDER:
        proj_ref[:, j * COL_BLOCK:(j + 1) * COL_BLOCK] = kind[j](j, project(j)).astype(BF16)


def _inproj(x2, ng, wlo, whi, wab, lng, lnb, tabc, tabs, alog, dtb, conv_w, *, seq, layer):
    t = x2.shape[0]
    tm = INPROJ_ROWS
    ncols = N_COL_BLOCKS * COL_BLOCK
    const = lambda i: (0, 0)
    rows = lambda i: (i, 0)
    lyr = lambda i: (layer, 0, 0)
    resident = pl.Buffered(1)
    return pl.pallas_call(
        functools.partial(_inproj_kernel, tiles_per_seq=seq // tm),
        grid=(t // tm,),
        in_specs=[
            pl.BlockSpec((tm, D_MODEL), rows),
            pl.BlockSpec((1, D_MODEL), const),
            pl.BlockSpec((None, D_MODEL, wlo.shape[2]), lyr, pipeline_mode=resident),
            pl.BlockSpec((None, D_MODEL, whi.shape[2]), lyr, pipeline_mode=resident),
            pl.BlockSpec((None, D_MODEL, LANES), lyr, pipeline_mode=resident),
            pl.BlockSpec((1, COL_BLOCK), const),
            pl.BlockSpec((1, COL_BLOCK), const),
            pl.BlockSpec((tm, LANES), rows),
            pl.BlockSpec((tm, LANES), rows),
            pl.BlockSpec((1, LANES), const),
            pl.BlockSpec((1, LANES), const),
            pl.BlockSpec((DN_CONV, 3 * DN_WIDTH), const),
        ],
        out_specs=[
            pl.BlockSpec((tm, ncols), rows),
            pl.BlockSpec((tm, LANES), rows),
        ],
        out_shape=[
            jax.ShapeDtypeStruct((t, ncols), BF16),
            jax.ShapeDtypeStruct((t, LANES), F32),
        ],
        scratch_shapes=[pltpu.VMEM((tm, D_MODEL), BF16),
                        pltpu.VMEM((3, CONV_PAD, COL_BLOCK), F32)],
        compiler_params=pltpu.CompilerParams(
            dimension_semantics=("arbitrary",), vmem_limit_bytes=VMEM_LIMIT),
        name="inproj",
    )(x2, ng, wlo, whi, wab, lng, lnb, tabc, tabs, alog, dtb, conv_w)


def _attn_kernel(q_ref, k_ref, v_ref, z_ref, lam_ref, sg_ref, o_ref, vt_ref, *, lambda_init, tq):
    seq = q_ref.shape[0]
    tk = tq
    nq = seq // tq

    lp = lam_ref[...]
    lam = (jnp.exp(jnp.sum(lp[0:1] * lp[1:2], axis=-1, keepdims=True))
           - jnp.exp(jnp.sum(lp[2:3] * lp[3:4], axis=-1, keepdims=True)) + lambda_init)

    for c0 in range(0, seq, tk):
        vt_ref[:, c0:c0 + tk] = v_ref[c0:c0 + tk, :].astype(F32).T.astype(BF16)

    def scores(qs, kb):
        ks = k_ref[kb * tk:(kb + 1) * tk, :]
        return [_nt_dot(ks, qs[c]) for c in range(2)]

    def online_update(st, m, l, a, vts):
        mn = jnp.maximum(m, jnp.max(st, axis=0, keepdims=True))
        alpha = jnp.exp2(m - mn)
        pt = jnp.exp2(st - mn)
        l = alpha * l + jnp.sum(pt, axis=0, keepdims=True)
        a = alpha * a + jnp.dot(vts, pt.astype(BF16), preferred_element_type=F32)
        return mn, l, a

    def softmax_pv(kb, sts, carry):
        vts = vt_ref[:, kb * tk:(kb + 1) * tk]
        return tuple(online_update(sts[c], *carry[c], vts) for c in range(2))

    half = tq // 2

    def diag_scores(qs, qi):
        ks = k_ref[qi * tk:(qi + 1) * tk, :]
        return [(_nt_dot(ks[:half, :], qs[c][:half, :]),
                 _nt_dot(ks, qs[c][half:, :])) for c in range(2)]

    def diag_softmax_pv(qi, dsc, carry):
        vts = vt_ref[:, qi * tk:(qi + 1) * tk]
        ri = lax.broadcasted_iota(jnp.int32, (half, half), 0)
        ci = lax.broadcasted_iota(jnp.int32, (half, half), 1)
        tri = ri <= ci
        out = []
        for c in range(2):
            m, l, a = carry[c]
            st_l, st_r = dsc[c]
            st_l = jnp.where(tri, st_l, -jnp.inf)
            st_r = jnp.concatenate([st_r[:half, :], jnp.where(tri, st_r[half:, :], -jnp.inf)], axis=0)
            ml, ll, al = online_update(st_l, m[:, :half], l[:, :half], a[:, :half], vts[:, :half])
            mr, lr, ar = online_update(st_r, m[:, half:], l[:, half:], a[:, half:], vts)
            out.append((jnp.concatenate([ml, mr], axis=1), jnp.concatenate([ll, lr], axis=1),
                        jnp.concatenate([al, ar], axis=1)))
        return tuple(out)

    for qi in range(nq):
        qr = slice(qi * tq, (qi + 1) * tq)
        q = q_ref[qr, :]
        lane = lax.broadcasted_iota(jnp.int32, q.shape, 1)
        zero = jnp.zeros_like(q)
        qs = (jnp.where(lane < DA_HEAD_DIM, q, zero), jnp.where(lane >= DA_HEAD_DIM, q, zero))
        carry = tuple((jnp.full((1, tq), -1e30, F32), jnp.zeros((1, tq), F32),
                       jnp.zeros((LANES, tq), F32)) for _ in range(2))
        sts = scores(qs, 0) if qi > 0 else None
        dsc = diag_scores(qs, qi) if qi == 0 else None
        for kb in range(qi):
            if kb + 1 < qi:
                nxt = scores(qs, kb + 1)
            else:
                nxt, dsc = None, diag_scores(qs, qi)
            carry = softmax_pv(kb, sts, carry)
            sts = nxt
        (_, l0, a0), (_, l1, a1) = diag_softmax_pv(qi, dsc, carry)

        ot = a0 / l0 - lam * (a1 / l1)
        ot = ot * lax.rsqrt(jnp.mean(ot * ot, axis=0, keepdims=True) + EPS)
        o = ot.T * sg_ref[...] * (1.0 - lambda_init)
        o_ref[qr, :] = (o * z_ref[qr, :].astype(F32)).astype(BF16)


def _attn(proj, lam_p, subln_g, *, batch, seq, lambda_init):
    t = proj.shape[0]
    per = COL_BLOCK // LANES
    tq = min(seq, 512)
    col = lambda cb: (lambda b, h: (b, cb * per + h))
    return pl.pallas_call(
        functools.partial(_attn_kernel, lambda_init=lambda_init, tq=tq),
        grid=(batch, DA_HEADS),
        in_specs=[
            pl.BlockSpec((seq, LANES), col(CB_DA_Q)),
            pl.BlockSpec((seq, LANES), col(CB_DA_K)),
            pl.BlockSpec((seq, LANES), col(CB_DA_V)),
            pl.BlockSpec((seq, LANES), col(CB_DA_Z)),
            pl.BlockSpec((8, LANES), lambda b, h: (0, 0)),
            pl.BlockSpec((1, LANES), lambda b, h: (0, 0)),
        ],
        out_specs=pl.BlockSpec((seq, LANES), lambda b, h: (b, h)),
        out_shape=jax.ShapeDtypeStruct((t, DA_HEADS * LANES), BF16),
        scratch_shapes=[pltpu.VMEM((LANES, seq), BF16)],
        compiler_params=pltpu.CompilerParams(
            dimension_semantics=("arbitrary", "arbitrary"), vmem_limit_bytes=VMEM_LIMIT),
        name="attn",
    )(proj, proj, proj, proj, lam_p, subln_g)


def _segment_cumsum(x, axis, seg):
    idx = lax.broadcasted_iota(jnp.int32, x.shape, axis) & (seg - 1)
    s = 1
    while s < seg:
        x = x + jnp.where(idx >= s, pltpu.roll(x, s, axis), 0.0)
        s *= 2
    return x


def _gdn_kernel(qkv_ref, z_ref, ab_ref, ng_ref, o_ref, st_ref):
    seq = qkv_ref.shape[0]
    cr = DN_CHUNK_ROWS
    ngroups = seq // cr
    nh = DN_HEADS

    st_ref[...] = jnp.zeros(st_ref.shape, F32)

    def load(r, part, hh):
        col = part * DN_WIDTH + hh * LANES
        return qkv_ref[r, col:col + LANES]

    def group(g, carry):
        r = pl.ds(pl.multiple_of(g * cr, cr), cr)
        abv = ab_ref[r, :]
        gcol = _segment_cumsum(abv, 0, cr)
        grow = gcol.T
        ri = lax.broadcasted_iota(jnp.int32, (cr, cr), 0)
        ci = lax.broadcasted_iota(jnp.int32, (cr, cr), 1)
        causal = ri >= ci
        strict = ri > ci
        heads = range(nh)
        dot = functools.partial(jnp.dot, preferred_element_type=F32)
        gcc = [gcol[:, hh:hh + 1] for hh in heads]
        bcol = [abv[:, nh + hh:nh + hh + 1] for hh in heads]
        decay = [jnp.exp(jnp.where(causal, gcc[hh] - grow[hh:hh + 1, :], -jnp.inf)) for hh in heads]
        q16 = [load(r, 0, hh) for hh in heads]
        k16 = [load(r, 1, hh) for hh in heads]
        qf = [q16[hh].astype(F32) for hh in heads]
        kf = [k16[hh].astype(F32) for hh in heads]
        vf = [load(r, 2, hh).astype(F32) for hh in heads]
        a = [jnp.where(strict, (bcol[hh] * _nt_dot(k16[hh], k16[hh])) * decay[hh], 0.0) for hh in heads]
        n = [-a[hh] for hh in heads]
        pw16 = [a[hh].astype(BF16) for hh in heads]
        for level in range(1, int(math.log2(cr))):
            lo = 2 ** level if 2 ** level >= BF16_SUBLANES else 0
            pw = [dot(pw16[hh][lo:, :], pw16[hh]) for hh in heads]
            pw16 = [pw[hh].astype(BF16) for hh in heads]
            if lo:
                pw16 = [jnp.concatenate([jnp.zeros((lo, cr), BF16), pw16[hh]], axis=0) for hh in heads]
            nlo = [n[hh][lo:, :] for hh in heads]
            nlo = [nlo[hh] + pw[hh] + dot(nlo[hh].astype(BF16), pw16[hh]) for hh in heads]
            n = [jnp.concatenate([n[hh][:lo, :], nlo[hh]], axis=0) if lo else nlo[hh] for hh in heads]
        eg = [jnp.exp(gcc[hh]) for hh in heads]
        rhs = [jnp.concatenate([vf[hh] * bcol[hh], kf[hh] * (bcol[hh] * eg[hh])], axis=1) for hh in heads]
        uw = [rhs[hh] + dot(n[hh].astype(BF16), rhs[hh].astype(BF16)) for hh in heads]
        qk = [_nt_dot(q16[hh], k16[hh]) * decay[hh] for hh in heads]
        gl = [gcc[hh][cr - 1:cr, :] for hh in heads]
        kdt = [(kf[hh] * jnp.exp(gl[hh] - gcc[hh])).T.astype(BF16) for hh in heads]
        st = [st_ref[hh] for hh in heads]
        st16 = [st[hh].astype(BF16) for hh in heads]
        v16 = [(uw[hh][:, :LANES] - dot(uw[hh][:, LANES:].astype(BF16), st16[hh])).astype(BF16) for hh in heads]
        o = [dot((qf[hh] * eg[hh]).astype(BF16), st16[hh]) + dot(qk[hh].astype(BF16), v16[hh]) for hh in heads]
        for hh in heads:
            st_ref[hh] = st[hh] * jnp.exp(gl[hh]) + dot(kdt[hh], v16[hh])
            oh = o[hh] * lax.rsqrt(jnp.mean(o[hh] * o[hh], axis=-1, keepdims=True) + EPS) * ng_ref[...]
            zc = z_ref[r, hh * LANES:(hh + 1) * LANES].astype(F32)
            o_ref[r, hh * LANES:(hh + 1) * LANES] = (oh * zc).astype(BF16)
        return carry

    lax.fori_loop(0, ngroups, group, 0, unroll=DN_GROUP_UNROLL)


def _gdn(proj, ab, norm_g, *, batch, seq):
    t = proj.shape[0]
    return pl.pallas_call(
        _gdn_kernel,
        grid=(batch,),
        in_specs=[
            pl.BlockSpec((seq, 3 * DN_WIDTH), lambda b: (b, CB_DN_QKV // 3)),
            pl.BlockSpec((seq, DN_WIDTH), lambda b: (b, CB_DN_Z)),
            pl.BlockSpec((seq, LANES), lambda b: (b, 0)),
            pl.BlockSpec((1, LANES), lambda b: (0, 0)),
        ],
        out_specs=pl.BlockSpec((seq, DN_WIDTH), lambda b: (b, 0)),
        out_shape=jax.ShapeDtypeStruct((t, DN_WIDTH), BF16),
        scratch_shapes=[pltpu.VMEM((DN_HEADS, DN_HEAD_DIM, DN_HEAD_DIM), F32)],
        compiler_params=pltpu.CompilerParams(
            dimension_semantics=("arbitrary",), vmem_limit_bytes=VMEM_LIMIT),
        name="gdn",
    )(proj, proj, ab, norm_g)


def _merge_kernel(x_ref, gates_ref, gm_ref, yb_ref, yc_ref, p_ref, ws_ref, bst_ref,
                  wa_ref, wb_ref, wc_ref, wo_ref, png_ref, wpg_ref, wpp_ref, fng_ref,
                  o_ref, ya_ref, *, rc, final):
    tm = x_ref.shape[0]

    ri = lax.broadcasted_iota(jnp.int32, (GM_CHUNK, GM_CHUNK), 0)
    ci = lax.broadcasted_iota(jnp.int32, (GM_CHUNK, GM_CHUNK), 1)
    wtri = [jnp.where(ri >= ci, ws_ref[g], 0.0).astype(BF16) for g in range(GM_GROUPS)]

    for c in range(tm // GM_CHUNK):
        r = slice(c * GM_CHUNK, (c + 1) * GM_CHUNK)
        for g in range(GM_GROUPS):
            cols = slice(g * LANES, (g + 1) * LANES)
            u = gm_ref[r, cols].astype(F32)
            v = gm_ref[r, GM_WIDTH + g * LANES:GM_WIDTH + (g + 1) * LANES]
            z = gm_ref[r, 2 * GM_WIDTH + g * LANES:2 * GM_WIDTH + (g + 1) * LANES].astype(F32)
            mix = jnp.dot(wtri[g], v, preferred_element_type=F32) + bst_ref[:, g:g + 1]
            ya_ref[r, cols] = (u * mix * z).astype(BF16)

    for c in range(tm // rc):
        r = slice(c * rc, (c + 1) * rc)
        ga = gates_ref[r, 0:D_MODEL].astype(F32)
        gb = gates_ref[r, D_MODEL:2 * D_MODEL].astype(F32)
        gc = gates_ref[r, 2 * D_MODEL:3 * D_MODEL].astype(F32)
        m = (ga * jnp.dot(ya_ref[r, :], wa_ref[...], preferred_element_type=F32)
             + gb * jnp.dot(yb_ref[r, :], wb_ref[...], preferred_element_type=F32)
             + gc * jnp.dot(yc_ref[r, :], wc_ref[...], preferred_element_type=F32))
        x1 = x_ref[r, :] + jnp.dot(m.astype(BF16), wo_ref[...], preferred_element_type=F32)
        hn = x1 * lax.rsqrt(jnp.mean(x1 * x1, axis=-1, keepdims=True) + EPS) * png_ref[...]
        gate = _sigmoid(jnp.dot(hn.astype(BF16), wpg_ref[...], preferred_element_type=F32))
        pp = jnp.dot(p_ref[r, :].astype(BF16), wpp_ref[...], preferred_element_type=F32)
        x2 = x1 + gate * pp
        if final:
            x2 = x2 * lax.rsqrt(jnp.mean(x2 * x2, axis=-1, keepdims=True) + EPS) * fng_ref[...]
        o_ref[r, :] = x2


def _merge(x2, proj, yb, yc, p3, ws, bst, wa, wb, wc, wo, png, wpg, wpp, fng, *, final, layer):
    t = x2.shape[0]
    tm = 512
    rc = 256
    rows = lambda i: (i, 0)
    const2 = lambda i: (0, 0)
    lyr = lambda i: (layer, 0, 0)
    resident = pl.Buffered(1)
    wspec = lambda k: pl.BlockSpec((None, k, D_MODEL), lyr, pipeline_mode=resident)
    return pl.pallas_call(
        functools.partial(_merge_kernel, rc=rc, final=final),
        grid=(t // tm,),
        in_specs=[
            pl.BlockSpec((tm, D_MODEL), rows),
            pl.BlockSpec((tm, 3 * D_MODEL), lambda i: (i, CB_GATES)),
            pl.BlockSpec((tm, 3 * GM_WIDTH), lambda i: (i, CB_GM_U // 3)),
            pl.BlockSpec((tm, DA_HEADS * LANES), rows),
            pl.BlockSpec((tm, DN_WIDTH), rows),
            pl.BlockSpec((None, tm, PLE_DIM), lambda i: (layer, i, 0)),
            pl.BlockSpec((None, GM_GROUPS, GM_CHUNK, GM_CHUNK), lambda i: (layer, 0, 0, 0)),
            pl.BlockSpec((GM_CHUNK, GM_GROUPS), const2),
            wspec(GM_WIDTH),
            wspec(DA_HEADS * LANES),
            wspec(DN_WIDTH),
            wspec(D_MODEL),
            pl.BlockSpec((1, D_MODEL), const2),
            wspec(D_MODEL),
            wspec(PLE_DIM),
            pl.BlockSpec((1, D_MODEL), const2),
        ],
        out_specs=pl.BlockSpec((tm, D_MODEL), rows),
        out_shape=jax.ShapeDtypeStruct((t, D_MODEL), F32),
        scratch_shapes=[pltpu.VMEM((tm, GM_WIDTH), BF16)],
        compiler_params=pltpu.CompilerParams(
            dimension_semantics=("arbitrary",), vmem_limit_bytes=VMEM_LIMIT),
        name="merge",
    )(x2, proj, proj, yb, yc, p3, ws, bst, wa, wb, wc, wo, png, wpg, wpp, fng)


def _split_w_in(w):
    w16 = w.astype(BF16)
    lo = w16[..., :W_IN_AB_START]
    hi = w16[..., W_IN_AB_START + 2 * DN_HEADS:]
    ab = jnp.pad(w16[..., W_IN_AB_START:W_IN_AB_START + 2 * DN_HEADS],
                 ((0, 0), (0, 0), (0, LANES - 2 * DN_HEADS)))
    return lo, hi, ab


def _rotary_tables(positions):
    inv_freq = ROPE_THETA ** (-jnp.arange(0, ROPE_DIM, 2, dtype=F32) / ROPE_DIM)
    ang = positions.astype(F32)[..., None] * inv_freq
    cos, sin = jnp.cos(ang), jnp.sin(ang)
    rest = DA_HEAD_DIM - ROPE_DIM
    tc = jnp.concatenate([cos, cos, jnp.ones(ang.shape[:-1] + (rest,), F32)], axis=-1)
    ts = jnp.concatenate([-sin, sin, jnp.zeros(ang.shape[:-1] + (rest,), F32)], axis=-1)
    rep = LANES // DA_HEAD_DIM
    tc = jnp.tile(tc, (1, 1, rep)).reshape(-1, LANES)
    ts = jnp.tile(ts, (1, 1, rep)).reshape(-1, LANES)
    return tc, ts


def _row(v, width=None):
    v = v.reshape(1, -1).astype(F32)
    if width is not None and v.shape[1] < width:
        v = jnp.pad(v, ((0, 0), (0, width - v.shape[1])))
    return v


def kernel(x, p, positions, norm_g, w_in, gm_ln_g, gm_ln_b, gm_ws, gm_bs, da_lq1, da_lk1, da_lq2, da_lk2, da_subln_g, dn_conv_w, dn_a_log, dn_dt_bias, dn_norm_g, w_br_a, w_br_b, w_br_c, w_out, ple_norm_g, w_ple_gate, w_ple_proj, final_norm_g):
    batch, seq, _ = x.shape
    depth = w_in.shape[0]
    t = batch * seq
    assert seq % DN_CHUNK_ROWS == 0 and seq % INPROJ_ROWS == 0

    w_lo, w_hi, w_ab = _split_w_in(w_in)
    tabc, tabs = _rotary_tables(positions)
    wa16, wb16, wc16 = w_br_a.astype(BF16), w_br_b.astype(BF16), w_br_c.astype(BF16)
    wo16, wpg16, wpp16 = w_out.astype(BF16), w_ple_gate.astype(BF16), w_ple_proj.astype(BF16)
    fng = _row(final_norm_g)
    p3 = p.reshape(depth, t, PLE_DIM)
    gm_ws = gm_ws.astype(F32)

    xc = x.reshape(t, D_MODEL)
    for i in range(depth):
        lambda_init = 0.8 - 0.6 * math.exp(-0.3 * i)
        proj, ab = _inproj(
            xc, _row(norm_g[i]), w_lo, w_hi, w_ab, _row(gm_ln_g[i]), _row(gm_ln_b[i]), tabc, tabs,
            _row(dn_a_log[i], LANES), _row(dn_dt_bias[i], LANES), dn_conv_w[i].astype(F32),
            seq=seq, layer=i)
        lam_p = jnp.pad(jnp.stack([da_lq1[i], da_lk1[i], da_lq2[i], da_lk2[i]]).astype(F32),
                        ((0, 4), (0, LANES - DA_HEAD_DIM)))
        yb = _attn(proj, lam_p, _row(da_subln_g[i]), batch=batch, seq=seq, lambda_init=lambda_init)
        yc = _gdn(proj, ab, _row(dn_norm_g[i]), batch=batch, seq=seq)
        xc = _merge(xc, proj, yb, yc, p3, gm_ws, gm_bs[i].T.astype(F32), wa16, wb16, wc16, wo16,
                    _row(ple_norm_g[i]), wpg16, wpp16, fng, final=(i == depth - 1), layer=i)
    return xc.reshape(batch, seq, D_MODEL)
```

```python
import functools
import math

import jax
import jax.numpy as jnp
from jax import lax
from jax.experimental import pallas as pl
from jax.experimental.pallas import tpu as pltpu

F32 = jnp.float32
BF16 = jnp.bfloat16

D_MODEL = 1024
PLE_DIM = 256
EPS = 1e-6
GM_GROUPS = 4
GM_CHUNK = 128
GM_WIDTH = 512
DA_HEADS = 4
DA_HEAD_DIM = 64
ROPE_THETA = 500000.0
ROPE_DIM = 16
DN_HEADS = 4
DN_HEAD_DIM = 128
DN_WIDTH = 512
DN_CONV = 4

LANES = 128
BF16_SUBLANES = 16
COL_BLOCK = 512
N_COL_BLOCKS = 17
CB_GATES = 0
CB_GM_U, CB_GM_V, CB_GM_Z = 6, 7, 8
CB_DN_QKV = 9
CB_DA_Q, CB_DA_K, CB_DA_V, CB_DA_Z = 12, 13, 14, 15
CB_DN_Z = 16
W_IN_AB_START = 3 * GM_WIDTH + 4 * 512 + 3 * DN_WIDTH
W_LO_BLOCK = {CB_GM_U: 0, CB_GM_V: 1, CB_GM_Z: 2, CB_DA_Q: 3, CB_DA_K: 4, CB_DA_V: 5, CB_DA_Z: 6,
              CB_DN_QKV: 7, CB_DN_QKV + 1: 8, CB_DN_QKV + 2: 9}
W_HI_BLOCK = {CB_DN_Z: 0, **{CB_GATES + g: 1 + g for g in range(6)}}
VMEM_LIMIT = 56 * 1024 * 1024
DN_CHUNK_ROWS = 256
DN_GROUP_UNROLL = 4
CONV_PAD = 8
LOG2_E = math.log2(math.e)
INPROJ_ROWS = 512
CAST_ROWS = 256
INPROJ_ORDER = (9, 0, 10, 1, 11, 2, 7, 3, 6, 4, 12, 5, 13, 14, 8, 15, 16)


def _sigmoid(x):
    return 0.5 * jnp.tanh(0.5 * x) + 0.5


def _gelu_tanh(x):
    c = math.sqrt(2.0 / math.pi)
    return 0.5 * x * (1.0 + jnp.tanh(c * (x + 0.044715 * (x * x * x))))


def _nt_dot(a, b):
    return lax.dot_general(a, b, (((1,), (1,)), ((), ())), preferred_element_type=F32)


def _inproj_kernel(x_ref, ng_ref, wlo_ref, whi_ref, wab_ref, lng_ref, lnb_ref, tc_ref, ts_ref,
                   alog_ref, dtb_ref, cw_ref, proj_ref, ab_ref, h_ref, tail_ref, *, tiles_per_seq):
    tm = x_ref.shape[0]

    xv = x_ref[...]
    ms = jnp.mean(xv * xv, axis=-1, keepdims=True)
    hb = (xv * lax.rsqrt(ms + EPS) * ng_ref[...]).astype(BF16)
    h_ref[...] = hb
    z = jnp.dot(hb, wab_ref[...], preferred_element_type=F32)
    lane = lax.broadcasted_iota(jnp.int32, z.shape, 1)
    zz = z + dtb_ref[...]
    sp = jnp.maximum(zz, 0.0) + jnp.log1p(jnp.exp(-jnp.abs(zz)))
    ab_ref[...] = jnp.where(lane < DN_HEADS, -jnp.exp(alog_ref[...]) * sp, _sigmoid(z))

    @pl.when(pl.program_id(0) % tiles_per_seq == 0)
    def _():
        tail_ref[...] = jnp.zeros(tail_ref.shape, F32)

    def ep_sig(j, acc):
        return _sigmoid(acc)

    def ep_gelu(j, acc):
        return _gelu_tanh(acc)

    def ep_geln(j, acc):
        g = _gelu_tanh(acc)
        mu = jnp.mean(g, axis=-1, keepdims=True)
        gc = g - mu
        var = jnp.mean(gc * gc, axis=-1, keepdims=True)
        return gc * lax.rsqrt(var + EPS) * lng_ref[...] + lnb_ref[...]

    def ep_silu(j, acc):
        return acc * _sigmoid(acc)

    def ep_id(j, acc):
        return acc

    def ep_rot(j, acc):
        scale = DA_HEAD_DIM ** -0.5 * LOG2_E if j == CB_DA_Q else 1.0
        tcv = tc_ref[...] * scale
        tsv = ts_ref[...] * scale
        lane = lax.broadcasted_iota(jnp.int32, tcv.shape, 1)
        first_half = (lane & (DA_HEAD_DIM - 1)) < (ROPE_DIM // 2)
        outs = []
        for s in range(COL_BLOCK // LANES):
            a = acc[:, s * LANES:(s + 1) * LANES]
            up = pltpu.roll(a, LANES - ROPE_DIM // 2, 1)
            dn = pltpu.roll(a, ROPE_DIM // 2, 1)
            outs.append(a * tcv + jnp.where(first_half, up, dn) * tsv)
        return jnp.concatenate(outs, axis=1)

    def ep_dn(j, acc):
        part = j - CB_DN_QKV
        xa = jnp.concatenate([tail_ref[part], acc], axis=0)
        tail_ref[part] = acc[tm - CONV_PAD:, :]
        w = cw_ref[:, part * COL_BLOCK:(part + 1) * COL_BLOCK]
        y = acc * w[DN_CONV - 1:DN_CONV, :]
        for tap in range(DN_CONV - 1):
            lo = CONV_PAD - (DN_CONV - 1 - tap)
            y = y + xa[lo:lo + tm, :] * w[tap:tap + 1, :]
        y = y * _sigmoid(y)
        if part == 2:
            return y
        scale = DN_HEAD_DIM ** -0.5 if part == 0 else 1.0
        outs = []
        for s in range(COL_BLOCK // LANES):
            ys = y[:, s * LANES:(s + 1) * LANES]
            outs.append(ys * (lax.rsqrt(jnp.sum(ys * ys, axis=-1, keepdims=True) + EPS) * scale))
        return jnp.concatenate(outs, axis=1)

    kind = {CB_GM_U: ep_gelu, CB_GM_V: ep_geln, CB_GM_Z: ep_silu, CB_DA_Z: ep_silu, CB_DN_Z: ep_silu,
            CB_DA_V: ep_id, CB_DA_Q: ep_rot, CB_DA_K: ep_rot}
    kind.update({CB_GATES + g: ep_sig for g in range(6)})
    kind.update({CB_DN_QKV + g: ep_dn for g in range(3)})

    def project(j):
        src_ref, sb = (wlo_ref, W_LO_BLOCK[j]) if j in W_LO_BLOCK else (whi_ref, W_HI_BLOCK[j])
        wj = src_ref[:, sb * COL_BLOCK:(sb + 1) * COL_BLOCK]
        return jnp.dot(h_ref[...], wj, preferred_element_type=F32)

    for j in INPROJ_ORDER:
        proj_ref[:, j * COL_BLOCK:(j + 1) * COL_BLOCK] = kind[j](j, project(j)).astype(BF16)


def _inproj(x2, ng, wlo, whi, wab, lng, lnb, tabc, tabs, alog, dtb, conv_w, *, seq, layer):
    t = x2.shape[0]
    tm = INPROJ_ROWS
    ncols = N_COL_BLOCKS * COL_BLOCK
    const = lambda i: (0, 0)
    rows = lambda i: (i, 0)
    lyr = lambda i: (layer, 0, 0)
    resident = pl.Buffered(1)
    return pl.pallas_call(
        functools.partial(_inproj_kernel, tiles_per_seq=seq // tm),
        grid=(t // tm,),
        in_specs=[
            pl.BlockSpec((tm, D_MODEL), rows),
            pl.BlockSpec((1, D_MODEL), const),
            pl.BlockSpec((None, D_MODEL, wlo.shape[2]), lyr, pipeline_mode=resident),
            pl.BlockSpec((None, D_MODEL, whi.shape[2]), lyr, pipeline_mode=resident),
            pl.BlockSpec((None, D_MODEL, LANES), lyr, pipeline_mode=resident),
            pl.BlockSpec((1, COL_BLOCK), const),
            pl.BlockSpec((1, COL_BLOCK), const),
            pl.BlockSpec((tm, LANES), rows),
            pl.BlockSpec((tm, LANES), rows),
            pl.BlockSpec((1, LANES), const),
            pl.BlockSpec((1, LANES), const),
            pl.BlockSpec((DN_CONV, 3 * DN_WIDTH), const),
        ],
        out_specs=[
            pl.BlockSpec((tm, ncols), rows),
            pl.BlockSpec((tm, LANES), rows),
        ],
        out_shape=[
            jax.ShapeDtypeStruct((t, ncols), BF16),
            jax.ShapeDtypeStruct((t, LANES), F32),
        ],
        scratch_shapes=[pltpu.VMEM((tm, D_MODEL), BF16),
                        pltpu.VMEM((3, CONV_PAD, COL_BLOCK), F32)],
        compiler_params=pltpu.CompilerParams(
            dimension_semantics=("arbitrary",), vmem_limit_bytes=VMEM_LIMIT),
        name="inproj",
    )(x2, ng, wlo, whi, wab, lng, lnb, tabc, tabs, alog, dtb, conv_w)


def _attn_kernel(q_ref, k_ref, v_ref, z_ref, lam_ref, sg_ref, o_ref, vt_ref, *, lambda_init, tq):
    seq = q_ref.shape[0]
    tk = tq
    nq = seq // tq

    lp = lam_ref[...]
    lam = (jnp.exp(jnp.sum(lp[0:1] * lp[1:2], axis=-1, keepdims=True))
           - jnp.exp(jnp.sum(lp[2:3] * lp[3:4], axis=-1, keepdims=True)) + lambda_init)

    for c0 in range(0, seq, tk):
        vt_ref[:, c0:c0 + tk] = v_ref[c0:c0 + tk, :].astype(F32).T.astype(BF16)

    def scores(qs, kb):
        ks = k_ref[kb * tk:(kb + 1) * tk, :]
        return [_nt_dot(ks, qs[c]) for c in range(2)]

    def online_update(st, m, l, a, vts):
        mn = jnp.maximum(m, jnp.max(st, axis=0, keepdims=True))
        alpha = jnp.exp2(m - mn)
        pt = jnp.exp2(st - mn)
        l = alpha * l + jnp.sum(pt, axis=0, keepdims=True)
        a = alpha * a + jnp.dot(vts, pt.astype(BF16), preferred_element_type=F32)
        return mn, l, a

    def softmax_pv(kb, sts, carry):
        vts = vt_ref[:, kb * tk:(kb + 1) * tk]
        return tuple(online_update(sts[c], *carry[c], vts) for c in range(2))

    half = tq // 2

    def diag_scores(qs, qi):
        ks = k_ref[qi * tk:(qi + 1) * tk, :]
        return [(_nt_dot(ks[:half, :], qs[c][:half, :]),
                 _nt_dot(ks, qs[c][half:, :])) for c in range(2)]

    def diag_softmax_pv(qi, dsc, carry):
        vts = vt_ref[:, qi * tk:(qi + 1) * tk]
        ri = lax.broadcasted_iota(jnp.int32, (half, half), 0)
        ci = lax.broadcasted_iota(jnp.int32, (half, half), 1)
        tri = ri <= ci
        out = []
        for c in range(2):
            m, l, a = carry[c]
            st_l, st_r = dsc[c]
            st_l = jnp.where(tri, st_l, -jnp.inf)
            st_r = jnp.concatenate([st_r[:half, :], jnp.where(tri, st_r[half:, :], -jnp.inf)], axis=0)
            ml, ll, al = online_update(st_l, m[:, :half], l[:, :half], a[:, :half], vts[:, :half])
            mr, lr, ar = online_update(st_r, m[:, half:], l[:, half:], a[:, half:], vts)
            out.append((jnp.concatenate([ml, mr], axis=1), jnp.concatenate([ll, lr], axis=1),
                        jnp.concatenate([al, ar], axis=1)))
        return tuple(out)

    for qi in range(nq):
        qr = slice(qi * tq, (qi + 1) * tq)
        q = q_ref[qr, :]
        lane = lax.broadcasted_iota(jnp.int32, q.shape, 1)
        zero = jnp.zeros_like(q)
        qs = (jnp.where(lane < DA_HEAD_DIM, q, zero), jnp.where(lane >= DA_HEAD_DIM, q, zero))
        carry = tuple((jnp.full((1, tq), -1e30, F32), jnp.zeros((1, tq), F32),
                       jnp.zeros((LANES, tq), F32)) for _ in range(2))
        sts = scores(qs, 0) if qi > 0 else None
        dsc = diag_scores(qs, qi) if qi == 0 else None
        for kb in range(qi):
            if kb + 1 < qi:
                nxt = scores(qs, kb + 1)
            else:
                nxt, dsc = None, diag_scores(qs, qi)
            carry = softmax_pv(kb, sts, carry)
            sts = nxt
        (_, l0, a0), (_, l1, a1) = diag_softmax_pv(qi, dsc, carry)

        ot = a0 / l0 - lam * (a1 / l1)
        ot = ot * lax.rsqrt(jnp.mean(ot * ot, axis=0, keepdims=True) + EPS)
        o = ot.T * sg_ref[...] * (1.0 - lambda_init)
        o_ref[qr, :] = (o * z_ref[qr, :].astype(F32)).astype(BF16)


def _attn(proj, lam_p, subln_g, *, batch, seq, lambda_init):
    t = proj.shape[0]
    per = COL_BLOCK // LANES
    tq = min(seq, 512)
    col = lambda cb: (lambda b, h: (b, cb * per + h))
    return pl.pallas_call(
        functools.partial(_attn_kernel, lambda_init=lambda_init, tq=tq),
        grid=(batch, DA_HEADS),
        in_specs=[
            pl.BlockSpec((seq, LANES), col(CB_DA_Q)),
            pl.BlockSpec((seq, LANES), col(CB_DA_K)),
            pl.BlockSpec((seq, LANES), col(CB_DA_V)),
            pl.BlockSpec((seq, LANES), col(CB_DA_Z)),
            pl.BlockSpec((8, LANES), lambda b, h: (0, 0)),
            pl.BlockSpec((1, LANES), lambda b, h: (0, 0)),
        ],
        out_specs=pl.BlockSpec((seq, LANES), lambda b, h: (b, h)),
        out_shape=jax.ShapeDtypeStruct((t, DA_HEADS * LANES), BF16),
        scratch_shapes=[pltpu.VMEM((LANES, seq), BF16)],
        compiler_params=pltpu.CompilerParams(
            dimension_semantics=("arbitrary", "arbitrary"), vmem_limit_bytes=VMEM_LIMIT),
        name="attn",
    )(proj, proj, proj, proj, lam_p, subln_g)


def _segment_cumsum(x, axis, seg):
    idx = lax.broadcasted_iota(jnp.int32, x.shape, axis) & (seg - 1)
    s = 1
    while s < seg:
        x = x + jnp.where(idx >= s, pltpu.roll(x, s, axis), 0.0)
        s *= 2
    return x


def _gdn_kernel(qkv_ref, z_ref, ab_ref, ng_ref, o_ref, st_ref):
    seq = qkv_ref.shape[0]
    cr = DN_CHUNK_ROWS
    ngroups = seq // cr
    nh = DN_HEADS

    st_ref[...] = jnp.zeros(st_ref.shape, F32)

    def load(r, part, hh):
        col = part * DN_WIDTH + hh * LANES
        return qkv_ref[r, col:col + LANES]

    def group(g, carry):
        r = pl.ds(pl.multiple_of(g * cr, cr), cr)
        abv = ab_ref[r, :]
        gcol = _segment_cumsum(abv, 0, cr)
        grow = gcol.T
        ri = lax.broadcasted_iota(jnp.int32, (cr, cr), 0)
        ci = lax.broadcasted_iota(jnp.int32, (cr, cr), 1)
        causal = ri >= ci
        strict = ri > ci
        heads = range(nh)
        dot = functools.partial(jnp.dot, preferred_element_type=F32)
        gcc = [gcol[:, hh:hh + 1] for hh in heads]
        bcol = [abv[:, nh + hh:nh + hh + 1] for hh in heads]
        decay = [jnp.exp(jnp.where(causal, gcc[hh] - grow[hh:hh + 1, :], -jnp.inf)) for hh in heads]
        q16 = [load(r, 0, hh) for hh in heads]
        k16 = [load(r, 1, hh) for hh in heads]
        qf = [q16[hh].astype(F32) for hh in heads]
        kf = [k16[hh].astype(F32) for hh in heads]
        vf = [load(r, 2, hh).astype(F32) for hh in heads]
        a = [jnp.where(strict, (bcol[hh] * _nt_dot(k16[hh], k16[hh])) * decay[hh], 0.0) for hh in heads]
        n = [-a[hh] for hh in heads]
        pw16 = [a[hh].astype(BF16) for hh in heads]
        for level in range(1, int(math.log2(cr))):
            lo = 2 ** level if 2 ** level >= BF16_SUBLANES else 0
            pw = [dot(pw16[hh][lo:, :], pw16[hh]) for hh in heads]
            pw16 = [pw[hh].astype(BF16) for hh in heads]
            if lo:
                pw16 = [jnp.concatenate([jnp.zeros((lo, cr), BF16), pw16[hh]], axis=0) for hh in heads]
            nlo = [n[hh][lo:, :] for hh in heads]
            nlo = [nlo[hh] + pw[hh] + dot(nlo[hh].astype(BF16), pw16[hh]) for hh in heads]
            n = [jnp.concatenate([n[hh][:lo, :], nlo[hh]], axis=0) if lo else nlo[hh] for hh in heads]
        eg = [jnp.exp(gcc[hh]) for hh in heads]
        rhs = [jnp.concatenate([vf[hh] * bcol[hh], kf[hh] * (bcol[hh] * eg[hh])], axis=1) for hh in heads]
        uw = [rhs[hh] + dot(n[hh].astype(BF16), rhs[hh].astype(BF16)) for hh in heads]
        qk = [_nt_dot(q16[hh], k16[hh]) * decay[hh] for hh in heads]
        gl = [gcc[hh][cr - 1:cr, :] for hh in heads]
        kdt = [(kf[hh] * jnp.exp(gl[hh] - gcc[hh])).T.astype(BF16) for hh in heads]
        st = [st_ref[hh] for hh in heads]
        st16 = [st[hh].astype(BF16) for hh in heads]
        v16 = [(uw[hh][:, :LANES] - dot(uw[hh][:, LANES:].astype(BF16), st16[hh])).astype(BF16) for hh in heads]
        o = [dot((qf[hh] * eg[hh]).astype(BF16), st16[hh]) + dot(qk[hh].astype(BF16), v16[hh]) for hh in heads]
        for hh in heads:
            st_ref[hh] = st[hh] * jnp.exp(gl[hh]) + dot(kdt[hh], v16[hh])
            oh = o[hh] * lax.rsqrt(jnp.mean(o[hh] * o[hh], axis=-1, keepdims=True) + EPS) * ng_ref[...]
            zc = z_ref[r, hh * LANES:(hh + 1) * LANES].astype(F32)
            o_ref[r, hh * LANES:(hh + 1) * LANES] = (oh * zc).astype(BF16)
        return carry

    lax.fori_loop(0, ngroups, group, 0, unroll=DN_GROUP_UNROLL)


def _gdn(proj, ab, norm_g, *, batch, seq):
    t = proj.shape[0]
    return pl.pallas_call(
        _gdn_kernel,
        grid=(batch,),
        in_specs=[
            pl.BlockSpec((seq, 3 * DN_WIDTH), lambda b: (b, CB_DN_QKV // 3)),
            pl.BlockSpec((seq, DN_WIDTH), lambda b: (b, CB_DN_Z)),
            pl.BlockSpec((seq, LANES), lambda b: (b, 0)),
            pl.BlockSpec((1, LANES), lambda b: (0, 0)),
        ],
        out_specs=pl.BlockSpec((seq, DN_WIDTH), lambda b: (b, 0)),
        out_shape=jax.ShapeDtypeStruct((t, DN_WIDTH), BF16),
        scratch_shapes=[pltpu.VMEM((DN_HEADS, DN_HEAD_DIM, DN_HEAD_DIM), F32)],
        compiler_params=pltpu.CompilerParams(
            dimension_semantics=("arbitrary",), vmem_limit_bytes=VMEM_LIMIT),
        name="gdn",
    )(proj, proj, ab, norm_g)


def _merge_kernel(x_ref, gates_ref, gm_ref, yb_ref, yc_ref, p_ref, ws_ref, bst_ref,
                  wa_ref, wb_ref, wc_ref, wo_ref, png_ref, wpg_ref, wpp_ref, fng_ref,
                  o_ref, ya_ref, *, rc, final):
    tm = x_ref.shape[0]

    ri = lax.broadcasted_iota(jnp.int32, (GM_CHUNK, GM_CHUNK), 0)
    ci = lax.broadcasted_iota(jnp.int32, (GM_CHUNK, GM_CHUNK), 1)
    wtri = [jnp.where(ri >= ci, ws_ref[g], 0.0).astype(BF16) for g in range(GM_GROUPS)]

    for c in range(tm // GM_CHUNK):
        r = slice(c * GM_CHUNK, (c + 1) * GM_CHUNK)
        for g in range(GM_GROUPS):
            cols = slice(g * LANES, (g + 1) * LANES)
            u = gm_ref[r, cols].astype(F32)
            v = gm_ref[r, GM_WIDTH + g * LANES:GM_WIDTH + (g + 1) * LANES]
            z = gm_ref[r, 2 * GM_WIDTH + g * LANES:2 * GM_WIDTH + (g + 1) * LANES].astype(F32)
            mix = jnp.dot(wtri[g], v, preferred_element_type=F32) + bst_ref[:, g:g + 1]
            ya_ref[r, cols] = (u * mix * z).astype(BF16)

    for c in range(tm // rc):
        r = slice(c * rc, (c + 1) * rc)
        ga = gates_ref[r, 0:D_MODEL].astype(F32)
        gb = gates_ref[r, D_MODEL:2 * D_MODEL].astype(F32)
        gc = gates_ref[r, 2 * D_MODEL:3 * D_MODEL].astype(F32)
        m = (ga * jnp.dot(ya_ref[r, :], wa_ref[...], preferred_element_type=F32)
             + gb * jnp.dot(yb_ref[r, :], wb_ref[...], preferred_element_type=F32)
             + gc * jnp.dot(yc_ref[r, :], wc_ref[...], preferred_element_type=F32))
        x1 = x_ref[r, :] + jnp.dot(m.astype(BF16), wo_ref[...], preferred_element_type=F32)
        hn = x1 * lax.rsqrt(jnp.mean(x1 * x1, axis=-1, keepdims=True) + EPS) * png_ref[...]
        gate = _sigmoid(jnp.dot(hn.astype(BF16), wpg_ref[...], preferred_element_type=F32))
        pp = jnp.dot(p_ref[r, :].astype(BF16), wpp_ref[...], preferred_element_type=F32)
        x2 = x1 + gate * pp
        if final:
            x2 = x2 * lax.rsqrt(jnp.mean(x2 * x2, axis=-1, keepdims=True) + EPS) * fng_ref[...]
        o_ref[r, :] = x2


def _merge(x2, proj, yb, yc, p3, ws, bst, wa, wb, wc, wo, png, wpg, wpp, fng, *, final, layer):
    t = x2.shape[0]
    tm = 512
    rc = 256
    rows = lambda i: (i, 0)
    const2 = lambda i: (0, 0)
    lyr = lambda i: (layer, 0, 0)
    resident = pl.Buffered(1)
    wspec = lambda k: pl.BlockSpec((None, k, D_MODEL), lyr, pipeline_mode=resident)
    return pl.pallas_call(
        functools.partial(_merge_kernel, rc=rc, final=final),
        grid=(t // tm,),
        in_specs=[
            pl.BlockSpec((tm, D_MODEL), rows),
            pl.BlockSpec((tm, 3 * D_MODEL), lambda i: (i, CB_GATES)),
            pl.BlockSpec((tm, 3 * GM_WIDTH), lambda i: (i, CB_GM_U // 3)),
            pl.BlockSpec((tm, DA_HEADS * LANES), rows),
            pl.BlockSpec((tm, DN_WIDTH), rows),
            pl.BlockSpec((None, tm, PLE_DIM), lambda i: (layer, i, 0)),
            pl.BlockSpec((None, GM_GROUPS, GM_CHUNK, GM_CHUNK), lambda i: (layer, 0, 0, 0)),
            pl.BlockSpec((GM_CHUNK, GM_GROUPS), const2),
            wspec(GM_WIDTH),
            wspec(DA_HEADS * LANES),
            wspec(DN_WIDTH),
            wspec(D_MODEL),
            pl.BlockSpec((1, D_MODEL), const2),
            wspec(D_MODEL),
            wspec(PLE_DIM),
            pl.BlockSpec((1, D_MODEL), const2),
        ],
        out_specs=pl.BlockSpec((tm, D_MODEL), rows),
        out_shape=jax.ShapeDtypeStruct((t, D_MODEL), F32),
        scratch_shapes=[pltpu.VMEM((tm, GM_WIDTH), BF16)],
        compiler_params=pltpu.CompilerParams(
            dimension_semantics=("arbitrary",), vmem_limit_bytes=VMEM_LIMIT),
        name="merge",
    )(x2, proj, proj, yb, yc, p3, ws, bst, wa, wb, wc, wo, png, wpg, wpp, fng)


def _split_w_in_kernel(w_ref, lo_ref, hi_ref, ab_ref):
    lo_ref[...] = w_ref[:, :W_IN_AB_START].astype(BF16)
    hi_ref[...] = w_ref[:, W_IN_AB_START + 2 * DN_HEADS:].astype(BF16)
    ab = w_ref[:, W_IN_AB_START:W_IN_AB_START + LANES]
    lane = lax.broadcasted_iota(jnp.int32, ab.shape, 1)
    ab_ref[...] = jnp.where(lane < 2 * DN_HEADS, ab, 0.0).astype(BF16)


def _split_w_in(w):
    depth, d, n = w.shape
    n_hi = n - W_IN_AB_START - 2 * DN_HEADS
    blk = lambda width: pl.BlockSpec((None, CAST_ROWS, width), lambda l, i: (l, i, 0))
    return pl.pallas_call(
        _split_w_in_kernel,
        grid=(depth, d // CAST_ROWS),
        in_specs=[blk(n)],
        out_specs=[blk(W_IN_AB_START), blk(n_hi), blk(LANES)],
        out_shape=[jax.ShapeDtypeStruct((depth, d, W_IN_AB_START), BF16),
                   jax.ShapeDtypeStruct((depth, d, n_hi), BF16),
                   jax.ShapeDtypeStruct((depth, d, LANES), BF16)],
        compiler_params=pltpu.CompilerParams(dimension_semantics=("arbitrary", "arbitrary")),
        name="split_w_in",
    )(w)


def _cast_kernel(w_ref, o_ref):
    o_ref[...] = w_ref[...].astype(BF16)


def _to_bf16(w):
    depth, k, n = w.shape
    spec = pl.BlockSpec((None, k, n), lambda l: (l, 0, 0))
    return pl.pallas_call(
        _cast_kernel, grid=(depth,), in_specs=[spec], out_specs=spec,
        out_shape=jax.ShapeDtypeStruct(w.shape, BF16),
        compiler_params=pltpu.CompilerParams(dimension_semantics=("arbitrary",)),
        name="to_bf16",
    )(w)


def _rotary_tables(positions):
    inv_freq = ROPE_THETA ** (-jnp.arange(0, ROPE_DIM, 2, dtype=F32) / ROPE_DIM)
    ang = positions.astype(F32)[..., None] * inv_freq
    cos, sin = jnp.cos(ang), jnp.sin(ang)
    rest = DA_HEAD_DIM - ROPE_DIM
    tc = jnp.concatenate([cos, cos, jnp.ones(ang.shape[:-1] + (rest,), F32)], axis=-1)
    ts = jnp.concatenate([-sin, sin, jnp.zeros(ang.shape[:-1] + (rest,), F32)], axis=-1)
    rep = LANES // DA_HEAD_DIM
    tc = jnp.tile(tc, (1, 1, rep)).reshape(-1, LANES)
    ts = jnp.tile(ts, (1, 1, rep)).reshape(-1, LANES)
    return tc, ts


def _row(v, width=None):
    v = v.reshape(1, -1).astype(F32)
    if width is not None and v.shape[1] < width:
        v = jnp.pad(v, ((0, 0), (0, width - v.shape[1])))
    return v


def kernel(x, p, positions, norm_g, w_in, gm_ln_g, gm_ln_b, gm_ws, gm_bs, da_lq1, da_lk1, da_lq2, da_lk2, da_subln_g, dn_conv_w, dn_a_log, dn_dt_bias, dn_norm_g, w_br_a, w_br_b, w_br_c, w_out, ple_norm_g, w_ple_gate, w_ple_proj, final_norm_g):
    batch, seq, _ = x.shape
    depth = w_in.shape[0]
    t = batch * seq
    assert seq % DN_CHUNK_ROWS == 0 and seq % INPROJ_ROWS == 0

    w_lo, w_hi, w_ab = _split_w_in(w_in)
    tabc, tabs = _rotary_tables(positions)
    wa16, wb16, wc16 = _to_bf16(w_br_a), _to_bf16(w_br_b), _to_bf16(w_br_c)
    wo16, wpg16, wpp16 = _to_bf16(w_out), _to_bf16(w_ple_gate), _to_bf16(w_ple_proj)
    fng = _row(final_norm_g)
    p3 = p.reshape(depth, t, PLE_DIM)
    gm_ws = gm_ws.astype(F32)

    xc = x.reshape(t, D_MODEL)
    for i in range(depth):
        lambda_init = 0.8 - 0.6 * math.exp(-0.3 * i)
        proj, ab = _inproj(
            xc, _row(norm_g[i]), w_lo, w_hi, w_ab, _row(gm_ln_g[i]), _row(gm_ln_b[i]), tabc, tabs,
            _row(dn_a_log[i], LANES), _row(dn_dt_bias[i], LANES), dn_conv_w[i].astype(F32),
            seq=seq, layer=i)
        lam_p = jnp.pad(jnp.stack([da_lq1[i], da_lk1[i], da_lq2[i], da_lk2[i]]).astype(F32),
                        ((0, 4), (0, LANES - DA_HEAD_DIM)))
        yb = _attn(proj, lam_p, _row(da_subln_g[i]), batch=batch, seq=seq, lambda_init=lambda_init)
        yc = _gdn(proj, ab, _row(dn_norm_g[i]), batch=batch, seq=seq)
        xc = _merge(xc, proj, yb, yc, p3, gm_ws, gm_bs[i].T.astype(F32), wa16, wb16, wc16, wo16,
                    _row(ple_norm_g[i]), wpg16, wpp16, fng, final=(i == depth - 1), layer=i)
    return xc.reshape(batch, seq, D_MODEL)
```

```python
import functools
import math

import jax
import jax.numpy as jnp
from jax import lax
from jax.experimental import pallas as pl
from jax.experimental.pallas import tpu as pltpu

F32 = jnp.float32
BF16 = jnp.bfloat16

D_MODEL = 1024
PLE_DIM = 256
EPS = 1e-6
GM_GROUPS = 4
GM_CHUNK = 128
GM_WIDTH = 512
DA_HEADS = 4
DA_HEAD_DIM = 64
ROPE_THETA = 500000.0
ROPE_DIM = 16
DN_HEADS = 4
DN_HEAD_DIM = 128
DN_WIDTH = 512
DN_CONV = 4

LANES = 128
BF16_SUBLANES = 16
COL_BLOCK = 512
N_COL_BLOCKS = 17
CB_GATES = 0
CB_GM_U, CB_GM_V, CB_GM_Z = 6, 7, 8
CB_DN_QKV = 9
CB_DA_Q, CB_DA_K, CB_DA_V, CB_DA_Z = 12, 13, 14, 15
CB_DN_Z = 16
W_IN_AB_START = 3 * GM_WIDTH + 4 * 512 + 3 * DN_WIDTH
W_LO_BLOCK = {CB_GM_U: 0, CB_GM_V: 1, CB_GM_Z: 2, CB_DA_Q: 3, CB_DA_K: 4, CB_DA_V: 5, CB_DA_Z: 6,
              CB_DN_QKV: 7, CB_DN_QKV + 1: 8, CB_DN_QKV + 2: 9}
W_HI_BLOCK = {CB_DN_Z: 0, **{CB_GATES + g: 1 + g for g in range(6)}}
VMEM_LIMIT = 56 * 1024 * 1024
DN_CHUNK_ROWS = 256
DN_GROUP_UNROLL = 4
CONV_PAD = 8
LOG2_E = math.log2(math.e)
INPROJ_ROWS = 512
ATTN_BLOCK = 512
MERGE_ROWS = 1024
MERGE_CHUNK = 512
INPROJ_ORDER = (9, 0, 10, 1, 11, 2, 7, 3, 6, 4, 12, 5, 13, 14, 8, 15, 16)


def _sigmoid(x):
    return 0.5 * jnp.tanh(0.5 * x) + 0.5


def _gelu_tanh(x):
    c = math.sqrt(2.0 / math.pi)
    return 0.5 * x * (1.0 + jnp.tanh(c * (x + 0.044715 * (x * x * x))))


def _nt_dot(a, b):
    return lax.dot_general(a, b, (((1,), (1,)), ((), ())), preferred_element_type=F32)


def _inproj_kernel(x_ref, ng_ref, wlo_ref, whi_ref, wab_ref, lng_ref, lnb_ref, tc_ref, ts_ref,
                   alog_ref, dtb_ref, cw_ref, proj_ref, ab_ref, h_ref, tail_ref, *, tiles_per_seq):
    tm = x_ref.shape[0]

    xv = x_ref[...]
    ms = jnp.mean(xv * xv, axis=-1, keepdims=True)
    hb = (xv * lax.rsqrt(ms + EPS) * ng_ref[...]).astype(BF16)
    h_ref[...] = hb
    z = jnp.dot(hb, wab_ref[...], preferred_element_type=F32)
    lane = lax.broadcasted_iota(jnp.int32, z.shape, 1)
    zz = z + dtb_ref[...]
    sp = jnp.maximum(zz, 0.0) + jnp.log1p(jnp.exp(-jnp.abs(zz)))
    ab_ref[...] = jnp.where(lane < DN_HEADS, -jnp.exp(alog_ref[...]) * sp, _sigmoid(z))

    @pl.when(pl.program_id(0) % tiles_per_seq == 0)
    def _():
        tail_ref[...] = jnp.zeros(tail_ref.shape, F32)

    def ep_sig(j, acc):
        return _sigmoid(acc)

    def ep_gelu(j, acc):
        return _gelu_tanh(acc)

    def ep_geln(j, acc):
        g = _gelu_tanh(acc)
        mu = jnp.mean(g, axis=-1, keepdims=True)
        gc = g - mu
        var = jnp.mean(gc * gc, axis=-1, keepdims=True)
        return gc * lax.rsqrt(var + EPS) * lng_ref[...] + lnb_ref[...]

    def ep_silu(j, acc):
        return acc * _sigmoid(acc)

    def ep_id(j, acc):
        return acc

    def ep_rot(j, acc):
        scale = DA_HEAD_DIM ** -0.5 * LOG2_E if j == CB_DA_Q else 1.0
        tcv = tc_ref[...] * scale
        tsv = ts_ref[...] * scale
        lane = lax.broadcasted_iota(jnp.int32, tcv.shape, 1)
        first_half = (lane & (DA_HEAD_DIM - 1)) < (ROPE_DIM // 2)
        outs = []
        for s in range(COL_BLOCK // LANES):
            a = acc[:, s * LANES:(s + 1) * LANES]
            up = pltpu.roll(a, LANES - ROPE_DIM // 2, 1)
            dn = pltpu.roll(a, ROPE_DIM // 2, 1)
            outs.append(a * tcv + jnp.where(first_half, up, dn) * tsv)
        return jnp.concatenate(outs, axis=1)

    def ep_dn(j, acc):
        part = j - CB_DN_QKV
        xa = jnp.concatenate([tail_ref[part], acc], axis=0)
        tail_ref[part] = acc[tm - CONV_PAD:, :]
        w = cw_ref[:, part * COL_BLOCK:(part + 1) * COL_BLOCK]
        y = acc * w[DN_CONV - 1:DN_CONV, :]
        for tap in range(DN_CONV - 1):
            lo = CONV_PAD - (DN_CONV - 1 - tap)
            y = y + xa[lo:lo + tm, :] * w[tap:tap + 1, :]
        y = y * _sigmoid(y)
        if part == 2:
            return y
        scale = DN_HEAD_DIM ** -0.5 if part == 0 else 1.0
        outs = []
        for s in range(COL_BLOCK // LANES):
            ys = y[:, s * LANES:(s + 1) * LANES]
            outs.append(ys * (lax.rsqrt(jnp.sum(ys * ys, axis=-1, keepdims=True) + EPS) * scale))
        return jnp.concatenate(outs, axis=1)

    kind = {CB_GM_U: ep_gelu, CB_GM_V: ep_geln, CB_GM_Z: ep_silu, CB_DA_Z: ep_silu, CB_DN_Z: ep_silu,
            CB_DA_V: ep_id, CB_DA_Q: ep_rot, CB_DA_K: ep_rot}
    kind.update({CB_GATES + g: ep_sig for g in range(6)})
    kind.update({CB_DN_QKV + g: ep_dn for g in range(3)})

    def project(j):
        src_ref, sb = (wlo_ref, W_LO_BLOCK[j]) if j in W_LO_BLOCK else (whi_ref, W_HI_BLOCK[j])
        wj = src_ref[:, sb * COL_BLOCK:(sb + 1) * COL_BLOCK]
        return jnp.dot(h_ref[...], wj, preferred_element_type=F32)

    for j in INPROJ_ORDER:
        proj_ref[:, j * COL_BLOCK:(j + 1) * COL_BLOCK] = kind[j](j, project(j)).astype(BF16)


def _inproj(x2, ng, wlo, whi, wab, lng, lnb, tabc, tabs, alog, dtb, conv_w, *, seq, layer):
    t = x2.shape[0]
    tm = INPROJ_ROWS
    ncols = N_COL_BLOCKS * COL_BLOCK
    const = lambda i: (0, 0)
    rows = lambda i: (i, 0)
    lyr = lambda i: (layer, 0, 0)
    resident = pl.Buffered(1)
    return pl.pallas_call(
        functools.partial(_inproj_kernel, tiles_per_seq=seq // tm),
        grid=(t // tm,),
        in_specs=[
            pl.BlockSpec((tm, D_MODEL), rows),
            pl.BlockSpec((1, D_MODEL), const),
            pl.BlockSpec((None, D_MODEL, wlo.shape[2]), lyr, pipeline_mode=resident),
            pl.BlockSpec((None, D_MODEL, whi.shape[2]), lyr, pipeline_mode=resident),
            pl.BlockSpec((None, D_MODEL, LANES), lyr, pipeline_mode=resident),
            pl.BlockSpec((1, COL_BLOCK), const),
            pl.BlockSpec((1, COL_BLOCK), const),
            pl.BlockSpec((tm, LANES), rows),
            pl.BlockSpec((tm, LANES), rows),
            pl.BlockSpec((1, LANES), const),
            pl.BlockSpec((1, LANES), const),
            pl.BlockSpec((DN_CONV, 3 * DN_WIDTH), const),
        ],
        out_specs=[
            pl.BlockSpec((tm, ncols), rows),
            pl.BlockSpec((tm, LANES), rows),
        ],
        out_shape=[
            jax.ShapeDtypeStruct((t, ncols), BF16),
            jax.ShapeDtypeStruct((t, LANES), F32),
        ],
        scratch_shapes=[pltpu.VMEM((tm, D_MODEL), BF16),
                        pltpu.VMEM((3, CONV_PAD, COL_BLOCK), F32)],
        compiler_params=pltpu.CompilerParams(
            dimension_semantics=("arbitrary",), vmem_limit_bytes=VMEM_LIMIT),
        name="inproj",
    )(x2, ng, wlo, whi, wab, lng, lnb, tabc, tabs, alog, dtb, conv_w)


def _attn_kernel(q_ref, k_ref, v_ref, z_ref, lam_ref, sg_ref, o_ref, vt_ref, *, lambda_init, tq):
    seq = q_ref.shape[0]
    tk = tq
    nq = seq // tq

    lp = lam_ref[...]
    lam = (jnp.exp(jnp.sum(lp[0:1] * lp[1:2], axis=-1, keepdims=True))
           - jnp.exp(jnp.sum(lp[2:3] * lp[3:4], axis=-1, keepdims=True)) + lambda_init)

    for c0 in range(0, seq, tk):
        vt_ref[:, c0:c0 + tk] = v_ref[c0:c0 + tk, :].astype(F32).T.astype(BF16)

    def scores(qs, kb):
        ks = k_ref[kb * tk:(kb + 1) * tk, :]
        return [_nt_dot(ks, qs[c]) for c in range(2)]

    def online_update(st, m, l, a, vts):
        mn = jnp.maximum(m, jnp.max(st, axis=0, keepdims=True))
        alpha = jnp.exp2(m - mn)
        pt = jnp.exp2(st - mn)
        l = alpha * l + jnp.sum(pt, axis=0, keepdims=True)
        a = alpha * a + jnp.dot(vts, pt.astype(BF16), preferred_element_type=F32)
        return mn, l, a

    def softmax_pv(kb, sts, carry):
        vts = vt_ref[:, kb * tk:(kb + 1) * tk]
        return tuple(online_update(sts[c], *carry[c], vts) for c in range(2))

    half = tq // 2

    def diag_scores(qs, qi):
        ks = k_ref[qi * tk:(qi + 1) * tk, :]
        return [(_nt_dot(ks[:half, :], qs[c][:half, :]),
                 _nt_dot(ks, qs[c][half:, :])) for c in range(2)]

    def diag_softmax_pv(qi, dsc, carry):
        vts = vt_ref[:, qi * tk:(qi + 1) * tk]
        ri = lax.broadcasted_iota(jnp.int32, (half, half), 0)
        ci = lax.broadcasted_iota(jnp.int32, (half, half), 1)
        tri = ri <= ci
        out = []
        for c in range(2):
            m, l, a = carry[c]
            st_l, st_r = dsc[c]
            st_l = jnp.where(tri, st_l, -jnp.inf)
            st_r = jnp.concatenate([st_r[:half, :], jnp.where(tri, st_r[half:, :], -jnp.inf)], axis=0)
            ml, ll, al = online_update(st_l, m[:, :half], l[:, :half], a[:, :half], vts[:, :half])
            mr, lr, ar = online_update(st_r, m[:, half:], l[:, half:], a[:, half:], vts)
            out.append((jnp.concatenate([ml, mr], axis=1), jnp.concatenate([ll, lr], axis=1),
                        jnp.concatenate([al, ar], axis=1)))
        return tuple(out)

    for qi in range(nq):
        qr = slice(qi * tq, (qi + 1) * tq)
        q = q_ref[qr, :]
        lane = lax.broadcasted_iota(jnp.int32, q.shape, 1)
        zero = jnp.zeros_like(q)
        qs = (jnp.where(lane < DA_HEAD_DIM, q, zero), jnp.where(lane >= DA_HEAD_DIM, q, zero))
        carry = tuple((jnp.full((1, tq), -1e30, F32), jnp.zeros((1, tq), F32),
                       jnp.zeros((LANES, tq), F32)) for _ in range(2))
        sts = scores(qs, 0) if qi > 0 else None
        dsc = diag_scores(qs, qi) if qi == 0 else None
        for kb in range(qi):
            if kb + 1 < qi:
                nxt = scores(qs, kb + 1)
            else:
                nxt, dsc = None, diag_scores(qs, qi)
            carry = softmax_pv(kb, sts, carry)
            sts = nxt
        (_, l0, a0), (_, l1, a1) = diag_softmax_pv(qi, dsc, carry)

        ot = a0 / l0 - lam * (a1 / l1)
        ot = ot * lax.rsqrt(jnp.mean(ot * ot, axis=0, keepdims=True) + EPS)
        o = ot.T * sg_ref[...] * (1.0 - lambda_init)
        o_ref[qr, :] = (o * z_ref[qr, :].astype(F32)).astype(BF16)


def _attn(proj, lam_p, subln_g, *, batch, seq, lambda_init):
    t = proj.shape[0]
    per = COL_BLOCK // LANES
    tq = min(seq, ATTN_BLOCK)
    col = lambda cb: (lambda b, h: (b, cb * per + h))
    return pl.pallas_call(
        functools.partial(_attn_kernel, lambda_init=lambda_init, tq=tq),
        grid=(batch, DA_HEADS),
        in_specs=[
            pl.BlockSpec((seq, LANES), col(CB_DA_Q)),
            pl.BlockSpec((seq, LANES), col(CB_DA_K)),
            pl.BlockSpec((seq, LANES), col(CB_DA_V)),
            pl.BlockSpec((seq, LANES), col(CB_DA_Z)),
            pl.BlockSpec((8, LANES), lambda b, h: (0, 0)),
            pl.BlockSpec((1, LANES), lambda b, h: (0, 0)),
        ],
        out_specs=pl.BlockSpec((seq, LANES), lambda b, h: (b, h)),
        out_shape=jax.ShapeDtypeStruct((t, DA_HEADS * LANES), BF16),
        scratch_shapes=[pltpu.VMEM((LANES, seq), BF16)],
        compiler_params=pltpu.CompilerParams(
            dimension_semantics=("arbitrary", "arbitrary"), vmem_limit_bytes=VMEM_LIMIT),
        name="attn",
    )(proj, proj, proj, proj, lam_p, subln_g)


def _segment_cumsum(x, axis, seg):
    idx = lax.broadcasted_iota(jnp.int32, x.shape, axis) & (seg - 1)
    s = 1
    while s < seg:
        x = x + jnp.where(idx >= s, pltpu.roll(x, s, axis), 0.0)
        s *= 2
    return x


def _gdn_kernel(qkv_ref, z_ref, ab_ref, ng_ref, o_ref, st_ref):
    seq = qkv_ref.shape[0]
    cr = DN_CHUNK_ROWS
    ngroups = seq // cr
    nh = DN_HEADS

    st_ref[...] = jnp.zeros(st_ref.shape, F32)

    def load(r, part, hh):
        col = part * DN_WIDTH + hh * LANES
        return qkv_ref[r, col:col + LANES]

    def group(g, carry):
        r = pl.ds(pl.multiple_of(g * cr, cr), cr)
        abv = ab_ref[r, :]
        gcol = _segment_cumsum(abv, 0, cr)
        grow = gcol.T
        ri = lax.broadcasted_iota(jnp.int32, (cr, cr), 0)
        ci = lax.broadcasted_iota(jnp.int32, (cr, cr), 1)
        causal = ri >= ci
        strict = ri > ci
        heads = range(nh)
        dot = functools.partial(jnp.dot, preferred_element_type=F32)
        gcc = [gcol[:, hh:hh + 1] for hh in heads]
        bcol = [abv[:, nh + hh:nh + hh + 1] for hh in heads]
        decay = [jnp.exp(jnp.where(causal, gcc[hh] - grow[hh:hh + 1, :], -jnp.inf)) for hh in heads]
        q16 = [load(r, 0, hh) for hh in heads]
        k16 = [load(r, 1, hh) for hh in heads]
        qf = [q16[hh].astype(F32) for hh in heads]
        kf = [k16[hh].astype(F32) for hh in heads]
        vf = [load(r, 2, hh).astype(F32) for hh in heads]
        a = [jnp.where(strict, (bcol[hh] * _nt_dot(k16[hh], k16[hh])) * decay[hh], 0.0) for hh in heads]
        n = [-a[hh] for hh in heads]
        pw16 = [a[hh].astype(BF16) for hh in heads]
        for level in range(1, int(math.log2(cr))):
            lo = 2 ** level if 2 ** level >= BF16_SUBLANES else 0
            pw = [dot(pw16[hh][lo:, :], pw16[hh]) for hh in heads]
            pw16 = [pw[hh].astype(BF16) for hh in heads]
            if lo:
                pw16 = [jnp.concatenate([jnp.zeros((lo, cr), BF16), pw16[hh]], axis=0) for hh in heads]
            nlo = [n[hh][lo:, :] for hh in heads]
            nlo = [nlo[hh] + pw[hh] + dot(nlo[hh].astype(BF16), pw16[hh]) for hh in heads]
            n = [jnp.concatenate([n[hh][:lo, :], nlo[hh]], axis=0) if lo else nlo[hh] for hh in heads]
        eg = [jnp.exp(gcc[hh]) for hh in heads]
        rhs = [jnp.concatenate([vf[hh] * bcol[hh], kf[hh] * (bcol[hh] * eg[hh])], axis=1) for hh in heads]
        uw = [rhs[hh] + dot(n[hh].astype(BF16), rhs[hh].astype(BF16)) for hh in heads]
        qk = [_nt_dot(q16[hh], k16[hh]) * decay[hh] for hh in heads]
        gl = [gcc[hh][cr - 1:cr, :] for hh in heads]
        kdt = [(kf[hh] * jnp.exp(gl[hh] - gcc[hh])).T.astype(BF16) for hh in heads]
        st = [st_ref[hh] for hh in heads]
        st16 = [st[hh].astype(BF16) for hh in heads]
        v16 = [(uw[hh][:, :LANES] - dot(uw[hh][:, LANES:].astype(BF16), st16[hh])).astype(BF16) for hh in heads]
        o = [dot((qf[hh] * eg[hh]).astype(BF16), st16[hh]) + dot(qk[hh].astype(BF16), v16[hh]) for hh in heads]
        for hh in heads:
            st_ref[hh] = st[hh] * jnp.exp(gl[hh]) + dot(kdt[hh], v16[hh])
            oh = o[hh] * lax.rsqrt(jnp.mean(o[hh] * o[hh], axis=-1, keepdims=True) + EPS) * ng_ref[...]
            zc = z_ref[r, hh * LANES:(hh + 1) * LANES].astype(F32)
            o_ref[r, hh * LANES:(hh + 1) * LANES] = (oh * zc).astype(BF16)
        return carry

    lax.fori_loop(0, ngroups, group, 0, unroll=DN_GROUP_UNROLL)


def _gdn(proj, ab, norm_g, *, batch, seq):
    t = proj.shape[0]
    return pl.pallas_call(
        _gdn_kernel,
        grid=(batch,),
        in_specs=[
            pl.BlockSpec((seq, 3 * DN_WIDTH), lambda b: (b, CB_DN_QKV // 3)),
            pl.BlockSpec((seq, DN_WIDTH), lambda b: (b, CB_DN_Z)),
            pl.BlockSpec((seq, LANES), lambda b: (b, 0)),
            pl.BlockSpec((1, LANES), lambda b: (0, 0)),
        ],
        out_specs=pl.BlockSpec((seq, DN_WIDTH), lambda b: (b, 0)),
        out_shape=jax.ShapeDtypeStruct((t, DN_WIDTH), BF16),
        scratch_shapes=[pltpu.VMEM((DN_HEADS, DN_HEAD_DIM, DN_HEAD_DIM), F32)],
        compiler_params=pltpu.CompilerParams(
            dimension_semantics=("arbitrary",), vmem_limit_bytes=VMEM_LIMIT),
        name="gdn",
    )(proj, proj, ab, norm_g)


def _merge_kernel(x_ref, gates_ref, gm_ref, yb_ref, yc_ref, p_ref, ws_ref, bst_ref,
                  wa_ref, wb_ref, wc_ref, wo_ref, png_ref, wpg_ref, wpp_ref, fng_ref,
                  o_ref, ya_ref, *, rc, final):
    tm = x_ref.shape[0]

    ri = lax.broadcasted_iota(jnp.int32, (GM_CHUNK, GM_CHUNK), 0)
    ci = lax.broadcasted_iota(jnp.int32, (GM_CHUNK, GM_CHUNK), 1)
    wtri = [jnp.where(ri >= ci, ws_ref[g], 0.0).astype(BF16) for g in range(GM_GROUPS)]

    for c in range(tm // GM_CHUNK):
        r = slice(c * GM_CHUNK, (c + 1) * GM_CHUNK)
        for g in range(GM_GROUPS):
            cols = slice(g * LANES, (g + 1) * LANES)
            u = gm_ref[r, cols].astype(F32)
            v = gm_ref[r, GM_WIDTH + g * LANES:GM_WIDTH + (g + 1) * LANES]
            z = gm_ref[r, 2 * GM_WIDTH + g * LANES:2 * GM_WIDTH + (g + 1) * LANES].astype(F32)
            mix = jnp.dot(wtri[g], v, preferred_element_type=F32) + bst_ref[:, g:g + 1]
            ya_ref[r, cols] = (u * mix * z).astype(BF16)

    for c in range(tm // rc):
        r = slice(c * rc, (c + 1) * rc)
        ga = gates_ref[r, 0:D_MODEL].astype(F32)
        gb = gates_ref[r, D_MODEL:2 * D_MODEL].astype(F32)
        gc = gates_ref[r, 2 * D_MODEL:3 * D_MODEL].astype(F32)
        m = (ga * jnp.dot(ya_ref[r, :], wa_ref[...], preferred_element_type=F32)
             + gb * jnp.dot(yb_ref[r, :], wb_ref[...], preferred_element_type=F32)
             + gc * jnp.dot(yc_ref[r, :], wc_ref[...], preferred_element_type=F32))
        x1 = x_ref[r, :] + jnp.dot(m.astype(BF16), wo_ref[...], preferred_element_type=F32)
        hn = x1 * lax.rsqrt(jnp.mean(x1 * x1, axis=-1, keepdims=True) + EPS) * png_ref[...]
        gate = _sigmoid(jnp.dot(hn.astype(BF16), wpg_ref[...], preferred_element_type=F32))
        pp = jnp.dot(p_ref[r, :].astype(BF16), wpp_ref[...], preferred_element_type=F32)
        x2 = x1 + gate * pp
        if final:
            x2 = x2 * lax.rsqrt(jnp.mean(x2 * x2, axis=-1, keepdims=True) + EPS) * fng_ref[...]
        o_ref[r, :] = x2


def _merge(x2, proj, yb, yc, p3, ws, bst, wa, wb, wc, wo, png, wpg, wpp, fng, *, final, layer):
    t = x2.shape[0]
    tm = min(t, MERGE_ROWS)
    rc = MERGE_CHUNK
    rows = lambda i: (i, 0)
    const2 = lambda i: (0, 0)
    lyr = lambda i: (layer, 0, 0)
    resident = pl.Buffered(1)
    wspec = lambda k: pl.BlockSpec((None, k, D_MODEL), lyr, pipeline_mode=resident)
    return pl.pallas_call(
        functools.partial(_merge_kernel, rc=rc, final=final),
        grid=(t // tm,),
        in_specs=[
            pl.BlockSpec((tm, D_MODEL), rows),
            pl.BlockSpec((tm, 3 * D_MODEL), lambda i: (i, CB_GATES)),
            pl.BlockSpec((tm, 3 * GM_WIDTH), lambda i: (i, CB_GM_U // 3)),
            pl.BlockSpec((tm, DA_HEADS * LANES), rows),
            pl.BlockSpec((tm, DN_WIDTH), rows),
            pl.BlockSpec((None, tm, PLE_DIM), lambda i: (layer, i, 0)),
            pl.BlockSpec((None, GM_GROUPS, GM_CHUNK, GM_CHUNK), lambda i: (layer, 0, 0, 0)),
            pl.BlockSpec((GM_CHUNK, GM_GROUPS), const2),
            wspec(GM_WIDTH),
            wspec(DA_HEADS * LANES),
            wspec(DN_WIDTH),
            wspec(D_MODEL),
            pl.BlockSpec((1, D_MODEL), const2),
            wspec(D_MODEL),
            wspec(PLE_DIM),
            pl.BlockSpec((1, D_MODEL), const2),
        ],
        out_specs=pl.BlockSpec((tm, D_MODEL), rows),
        out_shape=jax.ShapeDtypeStruct((t, D_MODEL), F32),
        scratch_shapes=[pltpu.VMEM((tm, GM_WIDTH), BF16)],
        compiler_params=pltpu.CompilerParams(
            dimension_semantics=("arbitrary",), vmem_limit_bytes=VMEM_LIMIT),
        name="merge",
    )(x2, proj, proj, yb, yc, p3, ws, bst, wa, wb, wc, wo, png, wpg, wpp, fng)


def _split_w_in(w):
    w16 = w.astype(BF16)
    lo = w16[..., :W_IN_AB_START]
    hi = w16[..., W_IN_AB_START + 2 * DN_HEADS:]
    ab = jnp.pad(w16[..., W_IN_AB_START:W_IN_AB_START + 2 * DN_HEADS],
                 ((0, 0), (0, 0), (0, LANES - 2 * DN_HEADS)))
    return lo, hi, ab


def _rotary_tables(positions):
    inv_freq = ROPE_THETA ** (-jnp.arange(0, ROPE_DIM, 2, dtype=F32) / ROPE_DIM)
    ang = positions.astype(F32)[..., None] * inv_freq
    cos, sin = jnp.cos(ang), jnp.sin(ang)
    rest = DA_HEAD_DIM - ROPE_DIM
    tc = jnp.concatenate([cos, cos, jnp.ones(ang.shape[:-1] + (rest,), F32)], axis=-1)
    ts = jnp.concatenate([-sin, sin, jnp.zeros(ang.shape[:-1] + (rest,), F32)], axis=-1)
    rep = LANES // DA_HEAD_DIM
    tc = jnp.tile(tc, (1, 1, rep)).reshape(-1, LANES)
    ts = jnp.tile(ts, (1, 1, rep)).reshape(-1, LANES)
    return tc, ts


def _row(v, width=None):
    v = v.reshape(1, -1).astype(F32)
    if width is not None and v.shape[1] < width:
        v = jnp.pad(v, ((0, 0), (0, width - v.shape[1])))
    return v


def kernel(x, p, positions, norm_g, w_in, gm_ln_g, gm_ln_b, gm_ws, gm_bs, da_lq1, da_lk1, da_lq2, da_lk2, da_subln_g, dn_conv_w, dn_a_log, dn_dt_bias, dn_norm_g, w_br_a, w_br_b, w_br_c, w_out, ple_norm_g, w_ple_gate, w_ple_proj, final_norm_g):
    batch, seq, _ = x.shape
    depth = w_in.shape[0]
    t = batch * seq
    assert seq % DN_CHUNK_ROWS == 0 and seq % INPROJ_ROWS == 0

    w_lo, w_hi, w_ab = _split_w_in(w_in)
    tabc, tabs = _rotary_tables(positions)
    wa16, wb16, wc16 = w_br_a.astype(BF16), w_br_b.astype(BF16), w_br_c.astype(BF16)
    wo16, wpg16, wpp16 = w_out.astype(BF16), w_ple_gate.astype(BF16), w_ple_proj.astype(BF16)
    fng = _row(final_norm_g)
    p3 = p.reshape(depth, t, PLE_DIM)
    gm_ws = gm_ws.astype(F32)

    xc = x.reshape(t, D_MODEL)
    for i in range(depth):
        lambda_init = 0.8 - 0.6 * math.exp(-0.3 * i)
        proj, ab = _inproj(
            xc, _row(norm_g[i]), w_lo, w_hi, w_ab, _row(gm_ln_g[i]), _row(gm_ln_b[i]), tabc, tabs,
            _row(dn_a_log[i], LANES), _row(dn_dt_bias[i], LANES), dn_conv_w[i].astype(F32),
            seq=seq, layer=i)
        lam_p = jnp.pad(jnp.stack([da_lq1[i], da_lk1[i], da_lq2[i], da_lk2[i]]).astype(F32),
                        ((0, 4), (0, LANES - DA_HEAD_DIM)))
        yb = _attn(proj, lam_p, _row(da_subln_g[i]), batch=batch, seq=seq, lambda_init=lambda_init)
        yc = _gdn(proj, ab, _row(dn_norm_g[i]), batch=batch, seq=seq)
        xc = _merge(xc, proj, yb, yc, p3, gm_ws, gm_bs[i].T.astype(F32), wa16, wb16, wc16, wo16,
                    _row(ple_norm_g[i]), wpg16, wpp16, fng, final=(i == depth - 1), layer=i)
    return xc.reshape(batch, seq, D_MODEL)
```

```python
import functools
import math

import jax
import jax.numpy as jnp
from jax import lax
from jax.experimental import pallas as pl
from jax.experimental.pallas import tpu as pltpu

F32 = jnp.float32
BF16 = jnp.bfloat16

D_MODEL = 1024
PLE_DIM = 256
EPS = 1e-6
GM_GROUPS = 4
GM_CHUNK = 128
GM_WIDTH = 512
DA_HEADS = 4
DA_HEAD_DIM = 64
ROPE_THETA = 500000.0
ROPE_DIM = 16
DN_HEADS = 4
DN_HEAD_DIM = 128
DN_WIDTH = 512
DN_CONV = 4

LANES = 128
BF16_SUBLANES = 16
COL_BLOCK = 512
N_COL_BLOCKS = 17
CB_GATES = 0
CB_GM_U, CB_GM_V, CB_GM_Z = 6, 7, 8
CB_DN_QKV = 9
CB_DA_Q, CB_DA_K, CB_DA_V, CB_DA_Z = 12, 13, 14, 15
CB_DN_Z = 16
W_IN_AB_START = 3 * GM_WIDTH + 4 * 512 + 3 * DN_WIDTH
W_LO_BLOCK = {CB_GM_U: 0, CB_GM_V: 1, CB_GM_Z: 2, CB_DA_Q: 3, CB_DA_K: 4, CB_DA_V: 5, CB_DA_Z: 6,
              CB_DN_QKV: 7, CB_DN_QKV + 1: 8, CB_DN_QKV + 2: 9}
W_HI_BLOCK = {CB_DN_Z: 0, **{CB_GATES + g: 1 + g for g in range(6)}}
VMEM_LIMIT = 56 * 1024 * 1024
DN_CHUNK_ROWS = 256
DN_GROUP_UNROLL = 4
CONV_PAD = 8
LOG2_E = math.log2(math.e)
INPROJ_ROWS = 512
ATTN_BLOCK = 512
MERGE_ROWS = 1024
MERGE_CHUNK = 512
INPROJ_ORDER = (9, 0, 10, 1, 11, 2, 7, 3, 6, 4, 12, 5, 13, 14, 8, 15, 16)


def _sigmoid(x):
    return 0.5 * jnp.tanh(0.5 * x) + 0.5


def _silu(x):
    m = 0.5 * x
    return m * jnp.tanh(m) + m


def _gelu_tanh(x):
    c = math.sqrt(2.0 / math.pi)
    h = 0.5 * x
    return h * jnp.tanh(x * (c + (c * 0.044715) * (x * x))) + h


def _nt_dot(a, b):
    return lax.dot_general(a, b, (((1,), (1,)), ((), ())), preferred_element_type=F32)


def _inproj_kernel(x_ref, ng_ref, wlo_ref, whi_ref, wab_ref, lng_ref, lnb_ref, tc_ref, ts_ref,
                   alog_ref, dtb_ref, cw_ref, proj_ref, ab_ref, h_ref, tail_ref, *, tiles_per_seq):
    tm = x_ref.shape[0]

    xv = x_ref[...]
    ms = jnp.mean(xv * xv, axis=-1, keepdims=True)
    hb = (xv * lax.rsqrt(ms + EPS) * ng_ref[...]).astype(BF16)
    h_ref[...] = hb
    z = jnp.dot(hb, wab_ref[...], preferred_element_type=F32)
    lane = lax.broadcasted_iota(jnp.int32, z.shape, 1)
    zz = z + dtb_ref[...]
    sp = jnp.maximum(zz, 0.0) + jnp.log1p(jnp.exp(-jnp.abs(zz)))
    ab_ref[...] = jnp.where(lane < DN_HEADS, -jnp.exp(alog_ref[...]) * sp, _sigmoid(z))

    @pl.when(pl.program_id(0) % tiles_per_seq == 0)
    def _():
        tail_ref[...] = jnp.zeros(tail_ref.shape, F32)

    def ep_sig(j, acc):
        return _sigmoid(acc)

    def ep_gelu(j, acc):
        return _gelu_tanh(acc)

    def ep_geln(j, acc):
        g = _gelu_tanh(acc)
        mu = jnp.mean(g, axis=-1, keepdims=True)
        gc = g - mu
        var = jnp.mean(gc * gc, axis=-1, keepdims=True)
        return gc * lax.rsqrt(var + EPS) * lng_ref[...] + lnb_ref[...]

    def ep_silu(j, acc):
        return _silu(acc)

    def ep_id(j, acc):
        return acc

    def ep_rot(j, acc):
        scale = DA_HEAD_DIM ** -0.5 * LOG2_E if j == CB_DA_Q else 1.0
        tcv = tc_ref[...] * scale
        tsv = ts_ref[...] * scale
        lane = lax.broadcasted_iota(jnp.int32, tcv.shape, 1)
        first_half = (lane & (DA_HEAD_DIM - 1)) < (ROPE_DIM // 2)
        outs = []
        for s in range(COL_BLOCK // LANES):
            a = acc[:, s * LANES:(s + 1) * LANES]
            up = pltpu.roll(a, LANES - ROPE_DIM // 2, 1)
            dn = pltpu.roll(a, ROPE_DIM // 2, 1)
            outs.append(a * tcv + jnp.where(first_half, up, dn) * tsv)
        return jnp.concatenate(outs, axis=1)

    def ep_dn(j, acc):
        part = j - CB_DN_QKV
        xa = jnp.concatenate([tail_ref[part], acc], axis=0)
        tail_ref[part] = acc[tm - CONV_PAD:, :]
        w = cw_ref[:, part * COL_BLOCK:(part + 1) * COL_BLOCK]
        back2 = pltpu.roll(xa, 2, 0)
        v = xa * w[2:3, :] + back2 * w[0:1, :]
        y = acc * w[3:4, :] + back2[CONV_PAD:, :] * w[1:2, :] + pltpu.roll(v, 1, 0)[CONV_PAD:, :]
        y = _silu(y)
        if part == 2:
            return y
        scale = DN_HEAD_DIM ** -0.5 if part == 0 else 1.0
        outs = []
        for s in range(COL_BLOCK // LANES):
            ys = y[:, s * LANES:(s + 1) * LANES]
            outs.append(ys * (lax.rsqrt(jnp.sum(ys * ys, axis=-1, keepdims=True) + EPS) * scale))
        return jnp.concatenate(outs, axis=1)

    kind = {CB_GM_U: ep_gelu, CB_GM_V: ep_geln, CB_GM_Z: ep_silu, CB_DA_Z: ep_silu, CB_DN_Z: ep_silu,
            CB_DA_V: ep_id, CB_DA_Q: ep_rot, CB_DA_K: ep_rot}
    kind.update({CB_GATES + g: ep_sig for g in range(6)})
    kind.update({CB_DN_QKV + g: ep_dn for g in range(3)})

    def project(j):
        src_ref, sb = (wlo_ref, W_LO_BLOCK[j]) if j in W_LO_BLOCK else (whi_ref, W_HI_BLOCK[j])
        wj = src_ref[:, sb * COL_BLOCK:(sb + 1) * COL_BLOCK]
        return jnp.dot(h_ref[...], wj, preferred_element_type=F32)

    for j in INPROJ_ORDER:
        proj_ref[:, j * COL_BLOCK:(j + 1) * COL_BLOCK] = kind[j](j, project(j)).astype(BF16)


def _inproj(x2, ng, wlo, whi, wab, lng, lnb, tabc, tabs, alog, dtb, conv_w, *, seq, layer):
    t = x2.shape[0]
    tm = INPROJ_ROWS
    ncols = N_COL_BLOCKS * COL_BLOCK
    const = lambda i: (0, 0)
    rows = lambda i: (i, 0)
    lyr = lambda i: (layer, 0, 0)
    resident = pl.Buffered(1)
    return pl.pallas_call(
        functools.partial(_inproj_kernel, tiles_per_seq=seq // tm),
        grid=(t // tm,),
        in_specs=[
            pl.BlockSpec((tm, D_MODEL), rows),
            pl.BlockSpec((1, D_MODEL), const),
            pl.BlockSpec((None, D_MODEL, wlo.shape[2]), lyr, pipeline_mode=resident),
            pl.BlockSpec((None, D_MODEL, whi.shape[2]), lyr, pipeline_mode=resident),
            pl.BlockSpec((None, D_MODEL, LANES), lyr, pipeline_mode=resident),
            pl.BlockSpec((1, COL_BLOCK), const),
            pl.BlockSpec((1, COL_BLOCK), const),
            pl.BlockSpec((tm, LANES), rows),
            pl.BlockSpec((tm, LANES), rows),
            pl.BlockSpec((1, LANES), const),
            pl.BlockSpec((1, LANES), const),
            pl.BlockSpec((DN_CONV, 3 * DN_WIDTH), const),
        ],
        out_specs=[
            pl.BlockSpec((tm, ncols), rows),
            pl.BlockSpec((tm, LANES), rows),
        ],
        out_shape=[
            jax.ShapeDtypeStruct((t, ncols), BF16),
            jax.ShapeDtypeStruct((t, LANES), F32),
        ],
        scratch_shapes=[pltpu.VMEM((tm, D_MODEL), BF16),
                        pltpu.VMEM((3, CONV_PAD, COL_BLOCK), F32)],
        compiler_params=pltpu.CompilerParams(
            dimension_semantics=("arbitrary",), vmem_limit_bytes=VMEM_LIMIT),
        name="inproj",
    )(x2, ng, wlo, whi, wab, lng, lnb, tabc, tabs, alog, dtb, conv_w)


def _attn_kernel(q_ref, k_ref, v_ref, z_ref, lam_ref, sg_ref, o_ref, vt_ref, *, lambda_init, tq):
    seq = q_ref.shape[0]
    tk = tq
    nq = seq // tq

    lp = lam_ref[...]
    lam = (jnp.exp(jnp.sum(lp[0:1] * lp[1:2], axis=-1, keepdims=True))
           - jnp.exp(jnp.sum(lp[2:3] * lp[3:4], axis=-1, keepdims=True)) + lambda_init)

    for c0 in range(0, seq, tk):
        vt_ref[:, c0:c0 + tk] = v_ref[c0:c0 + tk, :].astype(F32).T.astype(BF16)

    def scores(qs, kb):
        ks = k_ref[kb * tk:(kb + 1) * tk, :]
        return [_nt_dot(ks, qs[c]) for c in range(2)]

    def online_update(st, m, l, a, vts):
        mn = jnp.maximum(m, jnp.max(st, axis=0, keepdims=True))
        alpha = jnp.exp2(m - mn)
        pt = jnp.exp2(st - mn)
        l = alpha * l + jnp.sum(pt, axis=0, keepdims=True)
        a = alpha * a + jnp.dot(vts, pt.astype(BF16), preferred_element_type=F32)
        return mn, l, a

    def softmax_pv(kb, sts, carry):
        vts = vt_ref[:, kb * tk:(kb + 1) * tk]
        return tuple(online_update(sts[c], *carry[c], vts) for c in range(2))

    half = tq // 2

    def diag_scores(qs, qi):
        ks = k_ref[qi * tk:(qi + 1) * tk, :]
        return [(_nt_dot(ks[:half, :], qs[c][:half, :]),
                 _nt_dot(ks, qs[c][half:, :])) for c in range(2)]

    def diag_softmax_pv(qi, dsc, carry):
        vts = vt_ref[:, qi * tk:(qi + 1) * tk]
        ri = lax.broadcasted_iota(jnp.int32, (half, half), 0)
        ci = lax.broadcasted_iota(jnp.int32, (half, half), 1)
        tri = ri <= ci
        out = []
        for c in range(2):
            m, l, a = carry[c]
            st_l, st_r = dsc[c]
            st_l = jnp.where(tri, st_l, -jnp.inf)
            st_r = jnp.concatenate([st_r[:half, :], jnp.where(tri, st_r[half:, :], -jnp.inf)], axis=0)
            ml, ll, al = online_update(st_l, m[:, :half], l[:, :half], a[:, :half], vts[:, :half])
            mr, lr, ar = online_update(st_r, m[:, half:], l[:, half:], a[:, half:], vts)
            out.append((jnp.concatenate([ml, mr], axis=1), jnp.concatenate([ll, lr], axis=1),
                        jnp.concatenate([al, ar], axis=1)))
        return tuple(out)

    for qi in range(nq):
        qr = slice(qi * tq, (qi + 1) * tq)
        q = q_ref[qr, :]
        lane = lax.broadcasted_iota(jnp.int32, q.shape, 1)
        zero = jnp.zeros_like(q)
        qs = (jnp.where(lane < DA_HEAD_DIM, q, zero), jnp.where(lane >= DA_HEAD_DIM, q, zero))
        carry = tuple((jnp.full((1, tq), -1e30, F32), jnp.zeros((1, tq), F32),
                       jnp.zeros((LANES, tq), F32)) for _ in range(2))
        sts = scores(qs, 0) if qi > 0 else None
        dsc = diag_scores(qs, qi) if qi == 0 else None
        for kb in range(qi):
            if kb + 1 < qi:
                nxt = scores(qs, kb + 1)
            else:
                nxt, dsc = None, diag_scores(qs, qi)
            carry = softmax_pv(kb, sts, carry)
            sts = nxt
        (_, l0, a0), (_, l1, a1) = diag_softmax_pv(qi, dsc, carry)

        ot = a0 / l0 - lam * (a1 / l1)
        ot = ot * lax.rsqrt(jnp.mean(ot * ot, axis=0, keepdims=True) + EPS)
        o = ot.T * sg_ref[...] * (1.0 - lambda_init)
        o_ref[qr, :] = (o * z_ref[qr, :].astype(F32)).astype(BF16)


def _attn(proj, lam_p, subln_g, *, batch, seq, lambda_init):
    t = proj.shape[0]
    per = COL_BLOCK // LANES
    tq = min(seq, ATTN_BLOCK)
    col = lambda cb: (lambda b, h: (b, cb * per + h))
    return pl.pallas_call(
        functools.partial(_attn_kernel, lambda_init=lambda_init, tq=tq),
        grid=(batch, DA_HEADS),
        in_specs=[
            pl.BlockSpec((seq, LANES), col(CB_DA_Q)),
            pl.BlockSpec((seq, LANES), col(CB_DA_K)),
            pl.BlockSpec((seq, LANES), col(CB_DA_V)),
            pl.BlockSpec((seq, LANES), col(CB_DA_Z)),
            pl.BlockSpec((8, LANES), lambda b, h: (0, 0)),
            pl.BlockSpec((1, LANES), lambda b, h: (0, 0)),
        ],
        out_specs=pl.BlockSpec((seq, LANES), lambda b, h: (b, h)),
        out_shape=jax.ShapeDtypeStruct((t, DA_HEADS * LANES), BF16),
        scratch_shapes=[pltpu.VMEM((LANES, seq), BF16)],
        compiler_params=pltpu.CompilerParams(
            dimension_semantics=("arbitrary", "arbitrary"), vmem_limit_bytes=VMEM_LIMIT),
        name="attn",
    )(proj, proj, proj, proj, lam_p, subln_g)


def _segment_cumsum(x, axis, seg):
    idx = lax.broadcasted_iota(jnp.int32, x.shape, axis) & (seg - 1)
    s = 1
    while s < seg:
        x = x + jnp.where(idx >= s, pltpu.roll(x, s, axis), 0.0)
        s *= 2
    return x


def _gdn_kernel(qkv_ref, z_ref, ab_ref, ng_ref, o_ref, st_ref):
    seq = qkv_ref.shape[0]
    cr = DN_CHUNK_ROWS
    ngroups = seq // cr
    nh = DN_HEADS

    st_ref[...] = jnp.zeros(st_ref.shape, F32)

    def load(r, part, hh):
        col = part * DN_WIDTH + hh * LANES
        return qkv_ref[r, col:col + LANES]

    def group(g, carry):
        r = pl.ds(pl.multiple_of(g * cr, cr), cr)
        abv = ab_ref[r, :]
        gcol = _segment_cumsum(abv, 0, cr)
        grow = gcol.T
        ri = lax.broadcasted_iota(jnp.int32, (cr, cr), 0)
        ci = lax.broadcasted_iota(jnp.int32, (cr, cr), 1)
        causal = ri >= ci
        strict = ri > ci
        heads = range(nh)
        dot = functools.partial(jnp.dot, preferred_element_type=F32)
        gcc = [gcol[:, hh:hh + 1] for hh in heads]
        bcol = [abv[:, nh + hh:nh + hh + 1] for hh in heads]
        decay = [jnp.exp(jnp.where(causal, gcc[hh] - grow[hh:hh + 1, :], -jnp.inf)) for hh in heads]
        q16 = [load(r, 0, hh) for hh in heads]
        k16 = [load(r, 1, hh) for hh in heads]
        qf = [q16[hh].astype(F32) for hh in heads]
        kf = [k16[hh].astype(F32) for hh in heads]
        vf = [load(r, 2, hh).astype(F32) for hh in heads]
        a = [jnp.where(strict, (bcol[hh] * _nt_dot(k16[hh], k16[hh])) * decay[hh], 0.0) for hh in heads]
        n = [-a[hh] for hh in heads]
        pw16 = [a[hh].astype(BF16) for hh in heads]
        for level in range(1, int(math.log2(cr))):
            lo = 2 ** level if 2 ** level >= BF16_SUBLANES else 0
            pw = [dot(pw16[hh][lo:, :], pw16[hh]) for hh in heads]
            pw16 = [pw[hh].astype(BF16) for hh in heads]
            if lo:
                pw16 = [jnp.concatenate([jnp.zeros((lo, cr), BF16), pw16[hh]], axis=0) for hh in heads]
            nlo = [n[hh][lo:, :] for hh in heads]
            nlo = [nlo[hh] + pw[hh] + dot(nlo[hh].astype(BF16), pw16[hh]) for hh in heads]
            n = [jnp.concatenate([n[hh][:lo, :], nlo[hh]], axis=0) if lo else nlo[hh] for hh in heads]
        eg = [jnp.exp(gcc[hh]) for hh in heads]
        rhs = [jnp.concatenate([vf[hh] * bcol[hh], kf[hh] * (bcol[hh] * eg[hh])], axis=1) for hh in heads]
        uw = [rhs[hh] + dot(n[hh].astype(BF16), rhs[hh].astype(BF16)) for hh in heads]
        qk = [_nt_dot(q16[hh], k16[hh]) * decay[hh] for hh in heads]
        gl = [gcc[hh][cr - 1:cr, :] for hh in heads]
        kdt = [(kf[hh] * jnp.exp(gl[hh] - gcc[hh])).T.astype(BF16) for hh in heads]
        st = [st_ref[hh] for hh in heads]
        st16 = [st[hh].astype(BF16) for hh in heads]
        v16 = [(uw[hh][:, :LANES] - dot(uw[hh][:, LANES:].astype(BF16), st16[hh])).astype(BF16) for hh in heads]
        o = [dot((qf[hh] * eg[hh]).astype(BF16), st16[hh]) + dot(qk[hh].astype(BF16), v16[hh]) for hh in heads]
        for hh in heads:
            st_ref[hh] = st[hh] * jnp.exp(gl[hh]) + dot(kdt[hh], v16[hh])
            oh = o[hh] * lax.rsqrt(jnp.mean(o[hh] * o[hh], axis=-1, keepdims=True) + EPS) * ng_ref[...]
            zc = z_ref[r, hh * LANES:(hh + 1) * LANES].astype(F32)
            o_ref[r, hh * LANES:(hh + 1) * LANES] = (oh * zc).astype(BF16)
        return carry

    lax.fori_loop(0, ngroups, group, 0, unroll=DN_GROUP_UNROLL)


def _gdn(proj, ab, norm_g, *, batch, seq):
    t = proj.shape[0]
    return pl.pallas_call(
        _gdn_kernel,
        grid=(batch,),
        in_specs=[
            pl.BlockSpec((seq, 3 * DN_WIDTH), lambda b: (b, CB_DN_QKV // 3)),
            pl.BlockSpec((seq, DN_WIDTH), lambda b: (b, CB_DN_Z)),
            pl.BlockSpec((seq, LANES), lambda b: (b, 0)),
            pl.BlockSpec((1, LANES), lambda b: (0, 0)),
        ],
        out_specs=pl.BlockSpec((seq, DN_WIDTH), lambda b: (b, 0)),
        out_shape=jax.ShapeDtypeStruct((t, DN_WIDTH), BF16),
        scratch_shapes=[pltpu.VMEM((DN_HEADS, DN_HEAD_DIM, DN_HEAD_DIM), F32)],
        compiler_params=pltpu.CompilerParams(
            dimension_semantics=("arbitrary",), vmem_limit_bytes=VMEM_LIMIT),
        name="gdn",
    )(proj, proj, ab, norm_g)


def _merge_kernel(x_ref, gates_ref, gm_ref, yb_ref, yc_ref, p_ref, ws_ref, bst_ref,
                  wa_ref, wb_ref, wc_ref, wo_ref, png_ref, wpg_ref, wpp_ref, fng_ref,
                  o_ref, ya_ref, *, rc, final):
    tm = x_ref.shape[0]

    ri = lax.broadcasted_iota(jnp.int32, (GM_CHUNK, GM_CHUNK), 0)
    ci = lax.broadcasted_iota(jnp.int32, (GM_CHUNK, GM_CHUNK), 1)
    wtri = [jnp.where(ri >= ci, ws_ref[g], 0.0).astype(BF16) for g in range(GM_GROUPS)]

    for c in range(tm // GM_CHUNK):
        r = slice(c * GM_CHUNK, (c + 1) * GM_CHUNK)
        for g in range(GM_GROUPS):
            cols = slice(g * LANES, (g + 1) * LANES)
            u = gm_ref[r, cols].astype(F32)
            v = gm_ref[r, GM_WIDTH + g * LANES:GM_WIDTH + (g + 1) * LANES]
            z = gm_ref[r, 2 * GM_WIDTH + g * LANES:2 * GM_WIDTH + (g + 1) * LANES].astype(F32)
            mix = jnp.dot(wtri[g], v, preferred_element_type=F32) + bst_ref[:, g:g + 1]
            ya_ref[r, cols] = (u * mix * z).astype(BF16)

    for c in range(tm // rc):
        r = slice(c * rc, (c + 1) * rc)
        ga = gates_ref[r, 0:D_MODEL].astype(F32)
        gb = gates_ref[r, D_MODEL:2 * D_MODEL].astype(F32)
        gc = gates_ref[r, 2 * D_MODEL:3 * D_MODEL].astype(F32)
        m = (ga * jnp.dot(ya_ref[r, :], wa_ref[...], preferred_element_type=F32)
             + gb * jnp.dot(yb_ref[r, :], wb_ref[...], preferred_element_type=F32)
             + gc * jnp.dot(yc_ref[r, :], wc_ref[...], preferred_element_type=F32))
        x1 = x_ref[r, :] + jnp.dot(m.astype(BF16), wo_ref[...], preferred_element_type=F32)
        hn = x1 * lax.rsqrt(jnp.mean(x1 * x1, axis=-1, keepdims=True) + EPS) * png_ref[...]
        gate = _sigmoid(jnp.dot(hn.astype(BF16), wpg_ref[...], preferred_element_type=F32))
        pp = jnp.dot(p_ref[r, :].astype(BF16), wpp_ref[...], preferred_element_type=F32)
        x2 = x1 + gate * pp
        if final:
            x2 = x2 * lax.rsqrt(jnp.mean(x2 * x2, axis=-1, keepdims=True) + EPS) * fng_ref[...]
        o_ref[r, :] = x2


def _merge(x2, proj, yb, yc, p3, ws, bst, wa, wb, wc, wo, png, wpg, wpp, fng, *, final, layer):
    t = x2.shape[0]
    tm = min(t, MERGE_ROWS)
    rc = MERGE_CHUNK
    rows = lambda i: (i, 0)
    const2 = lambda i: (0, 0)
    lyr = lambda i: (layer, 0, 0)
    resident = pl.Buffered(1)
    wspec = lambda k: pl.BlockSpec((None, k, D_MODEL), lyr, pipeline_mode=resident)
    return pl.pallas_call(
        functools.partial(_merge_kernel, rc=rc, final=final),
        grid=(t // tm,),
        in_specs=[
            pl.BlockSpec((tm, D_MODEL), rows),
            pl.BlockSpec((tm, 3 * D_MODEL), lambda i: (i, CB_GATES)),
            pl.BlockSpec((tm, 3 * GM_WIDTH), lambda i: (i, CB_GM_U // 3)),
            pl.BlockSpec((tm, DA_HEADS * LANES), rows),
            pl.BlockSpec((tm, DN_WIDTH), rows),
            pl.BlockSpec((None, tm, PLE_DIM), lambda i: (layer, i, 0)),
            pl.BlockSpec((None, GM_GROUPS, GM_CHUNK, GM_CHUNK), lambda i: (layer, 0, 0, 0)),
            pl.BlockSpec((GM_CHUNK, GM_GROUPS), const2),
            wspec(GM_WIDTH),
            wspec(DA_HEADS * LANES),
            wspec(DN_WIDTH),
            wspec(D_MODEL),
            pl.BlockSpec((1, D_MODEL), const2),
            wspec(D_MODEL),
            wspec(PLE_DIM),
            pl.BlockSpec((1, D_MODEL), const2),
        ],
        out_specs=pl.BlockSpec((tm, D_MODEL), rows),
        out_shape=jax.ShapeDtypeStruct((t, D_MODEL), F32),
        scratch_shapes=[pltpu.VMEM((tm, GM_WIDTH), BF16)],
        compiler_params=pltpu.CompilerParams(
            dimension_semantics=("arbitrary",), vmem_limit_bytes=VMEM_LIMIT),
        name="merge",
    )(x2, proj, proj, yb, yc, p3, ws, bst, wa, wb, wc, wo, png, wpg, wpp, fng)


def _split_w_in(w):
    w16 = w.astype(BF16)
    lo = w16[..., :W_IN_AB_START]
    hi = w16[..., W_IN_AB_START + 2 * DN_HEADS:]
    ab = jnp.pad(w16[..., W_IN_AB_START:W_IN_AB_START + 2 * DN_HEADS],
                 ((0, 0), (0, 0), (0, LANES - 2 * DN_HEADS)))
    return lo, hi, ab


def _rotary_tables(positions):
    inv_freq = ROPE_THETA ** (-jnp.arange(0, ROPE_DIM, 2, dtype=F32) / ROPE_DIM)
    ang = positions.astype(F32)[..., None] * inv_freq
    cos, sin = jnp.cos(ang), jnp.sin(ang)
    rest = DA_HEAD_DIM - ROPE_DIM
    tc = jnp.concatenate([cos, cos, jnp.ones(ang.shape[:-1] + (rest,), F32)], axis=-1)
    ts = jnp.concatenate([-sin, sin, jnp.zeros(ang.shape[:-1] + (rest,), F32)], axis=-1)
    rep = LANES // DA_HEAD_DIM
    tc = jnp.tile(tc, (1, 1, rep)).reshape(-1, LANES)
    ts = jnp.tile(ts, (1, 1, rep)).reshape(-1, LANES)
    return tc, ts


def _row(v, width=None):
    v = v.reshape(1, -1).astype(F32)
    if width is not None and v.shape[1] < width:
        v = jnp.pad(v, ((0, 0), (0, width - v.shape[1])))
    return v


def kernel(x, p, positions, norm_g, w_in, gm_ln_g, gm_ln_b, gm_ws, gm_bs, da_lq1, da_lk1, da_lq2, da_lk2, da_subln_g, dn_conv_w, dn_a_log, dn_dt_bias, dn_norm_g, w_br_a, w_br_b, w_br_c, w_out, ple_norm_g, w_ple_gate, w_ple_proj, final_norm_g):
    batch, seq, _ = x.shape
    depth = w_in.shape[0]
    t = batch * seq
    assert seq % DN_CHUNK_ROWS == 0 and seq % INPROJ_ROWS == 0

    w_lo, w_hi, w_ab = _split_w_in(w_in)
    tabc, tabs = _rotary_tables(positions)
    wa16, wb16, wc16 = w_br_a.astype(BF16), w_br_b.astype(BF16), w_br_c.astype(BF16)
    wo16, wpg16, wpp16 = w_out.astype(BF16), w_ple_gate.astype(BF16), w_ple_proj.astype(BF16)
    fng = _row(final_norm_g)
    p3 = p.reshape(depth, t, PLE_DIM)
    gm_ws = gm_ws.astype(F32)

    xc = x.reshape(t, D_MODEL)
    for i in range(depth):
        lambda_init = 0.8 - 0.6 * math.exp(-0.3 * i)
        proj, ab = _inproj(
            xc, _row(norm_g[i]), w_lo, w_hi, w_ab, _row(gm_ln_g[i]), _row(gm_ln_b[i]), tabc, tabs,
            _row(dn_a_log[i], LANES), _row(dn_dt_bias[i], LANES), dn_conv_w[i].astype(F32),
            seq=seq, layer=i)
        lam_p = jnp.pad(jnp.stack([da_lq1[i], da_lk1[i], da_lq2[i], da_lk2[i]]).astype(F32),
                        ((0, 4), (0, LANES - DA_HEAD_DIM)))
        yb = _attn(proj, lam_p, _row(da_subln_g[i]), batch=batch, seq=seq, lambda_init=lambda_init)
        yc = _gdn(proj, ab, _row(dn_norm_g[i]), batch=batch, seq=seq)
        xc = _merge(xc, proj, yb, yc, p3, gm_ws, gm_bs[i].T.astype(F32), wa16, wb16, wc16, wo16,
                    _row(ple_norm_g[i]), wpg16, wpp16, fng, final=(i == depth - 1), layer=i)
    return xc.reshape(batch, seq, D_MODEL)
```

```python
import functools
import math

import jax
import jax.numpy as jnp
from jax import lax
from jax.experimental import pallas as pl
from jax.experimental.pallas import tpu as pltpu

F32 = jnp.float32
BF16 = jnp.bfloat16

D_MODEL = 1024
PLE_DIM = 256
EPS = 1e-6
GM_GROUPS = 4
GM_CHUNK = 128
GM_WIDTH = 512
DA_HEADS = 4
DA_HEAD_DIM = 64
ROPE_THETA = 500000.0
ROPE_DIM = 16
DN_HEADS = 4
DN_HEAD_DIM = 128
DN_WIDTH = 512
DN_CONV = 4

LANES = 128
BF16_SUBLANES = 16
COL_BLOCK = 512
N_COL_BLOCKS = 17
CB_GATES = 0
CB_GM_U, CB_GM_V, CB_GM_Z = 6, 7, 8
CB_DN_QKV = 9
CB_DA_Q, CB_DA_K, CB_DA_V, CB_DA_Z = 12, 13, 14, 15
CB_DN_Z = 16
W_IN_AB_START = 3 * GM_WIDTH + 4 * 512 + 3 * DN_WIDTH
W_LO_BLOCK = {CB_GM_U: 0, CB_GM_V: 1, CB_GM_Z: 2, CB_DA_Q: 3, CB_DA_K: 4, CB_DA_V: 5, CB_DA_Z: 6,
              CB_DN_QKV: 7, CB_DN_QKV + 1: 8, CB_DN_QKV + 2: 9}
W_HI_BLOCK = {CB_DN_Z: 0, **{CB_GATES + g: 1 + g for g in range(6)}}
VMEM_LIMIT = 56 * 1024 * 1024
DN_CHUNK_ROWS = 256
DN_GROUP_UNROLL = 4
CONV_PAD = 8
LOG2_E = math.log2(math.e)
INPROJ_ROWS = 512
ATTN_BLOCK = 512
ATTN_HEADS_PER_STEP = 2
MERGE_ROWS = 1024
MERGE_CHUNK = 512
INPROJ_ORDER = (9, 0, 10, 1, 11, 2, 7, 3, 6, 4, 12, 5, 13, 14, 8, 15, 16)


def _sigmoid(x):
    return 0.5 * jnp.tanh(0.5 * x) + 0.5


def _silu(x):
    m = 0.5 * x
    return m * jnp.tanh(m) + m


def _gelu_tanh(x):
    c = math.sqrt(2.0 / math.pi)
    h = 0.5 * x
    return h * jnp.tanh(x * (c + (c * 0.044715) * (x * x))) + h


def _nt_dot(a, b):
    return lax.dot_general(a, b, (((1,), (1,)), ((), ())), preferred_element_type=F32)


def _inproj_kernel(x_ref, ng_ref, wlo_ref, whi_ref, wab_ref, lng_ref, lnb_ref, tc_ref, ts_ref,
                   alog_ref, dtb_ref, cw_ref, proj_ref, ab_ref, h_ref, tail_ref, *, tiles_per_seq):
    tm = x_ref.shape[0]

    xv = x_ref[...]
    ms = jnp.mean(xv * xv, axis=-1, keepdims=True)
    hb = (xv * lax.rsqrt(ms + EPS) * ng_ref[...]).astype(BF16)
    h_ref[...] = hb
    z = jnp.dot(hb, wab_ref[...], preferred_element_type=F32)
    lane = lax.broadcasted_iota(jnp.int32, z.shape, 1)
    zz = z + dtb_ref[...]
    sp = jnp.maximum(zz, 0.0) + jnp.log1p(jnp.exp(-jnp.abs(zz)))
    ab_ref[...] = jnp.where(lane < DN_HEADS, -jnp.exp(alog_ref[...]) * sp, _sigmoid(z))

    @pl.when(pl.program_id(0) % tiles_per_seq == 0)
    def _():
        tail_ref[...] = jnp.zeros(tail_ref.shape, F32)

    def ep_sig(j, acc):
        return _sigmoid(acc)

    def ep_gelu(j, acc):
        return _gelu_tanh(acc)

    def ep_geln(j, acc):
        g = _gelu_tanh(acc)
        mu = jnp.mean(g, axis=-1, keepdims=True)
        gc = g - mu
        var = jnp.mean(gc * gc, axis=-1, keepdims=True)
        return gc * lax.rsqrt(var + EPS) * lng_ref[...] + lnb_ref[...]

    def ep_silu(j, acc):
        return _silu(acc)

    def ep_id(j, acc):
        return acc

    def ep_rot(j, acc):
        scale = DA_HEAD_DIM ** -0.5 * LOG2_E if j == CB_DA_Q else 1.0
        tcv = tc_ref[...] * scale
        tsv = ts_ref[...] * scale
        lane = lax.broadcasted_iota(jnp.int32, tcv.shape, 1)
        first_half = (lane & (DA_HEAD_DIM - 1)) < (ROPE_DIM // 2)
        outs = []
        for s in range(COL_BLOCK // LANES):
            a = acc[:, s * LANES:(s + 1) * LANES]
            up = pltpu.roll(a, LANES - ROPE_DIM // 2, 1)
            dn = pltpu.roll(a, ROPE_DIM // 2, 1)
            outs.append(a * tcv + jnp.where(first_half, up, dn) * tsv)
        return jnp.concatenate(outs, axis=1)

    def ep_dn(j, acc):
        part = j - CB_DN_QKV
        xa = jnp.concatenate([tail_ref[part], acc], axis=0)
        tail_ref[part] = acc[tm - CONV_PAD:, :]
        w = cw_ref[:, part * COL_BLOCK:(part + 1) * COL_BLOCK]
        back2 = pltpu.roll(xa, 2, 0)
        v = xa * w[2:3, :] + back2 * w[0:1, :]
        y = acc * w[3:4, :] + back2[CONV_PAD:, :] * w[1:2, :] + pltpu.roll(v, 1, 0)[CONV_PAD:, :]
        y = _silu(y)
        if part == 2:
            return y
        scale = DN_HEAD_DIM ** -0.5 if part == 0 else 1.0
        outs = []
        for s in range(COL_BLOCK // LANES):
            ys = y[:, s * LANES:(s + 1) * LANES]
            outs.append(ys * (lax.rsqrt(jnp.sum(ys * ys, axis=-1, keepdims=True) + EPS) * scale))
        return jnp.concatenate(outs, axis=1)

    kind = {CB_GM_U: ep_gelu, CB_GM_V: ep_geln, CB_GM_Z: ep_silu, CB_DA_Z: ep_silu, CB_DN_Z: ep_silu,
            CB_DA_V: ep_id, CB_DA_Q: ep_rot, CB_DA_K: ep_rot}
    kind.update({CB_GATES + g: ep_sig for g in range(6)})
    kind.update({CB_DN_QKV + g: ep_dn for g in range(3)})

    def project(j):
        src_ref, sb = (wlo_ref, W_LO_BLOCK[j]) if j in W_LO_BLOCK else (whi_ref, W_HI_BLOCK[j])
        wj = src_ref[:, sb * COL_BLOCK:(sb + 1) * COL_BLOCK]
        return jnp.dot(h_ref[...], wj, preferred_element_type=F32)

    for j in INPROJ_ORDER:
        proj_ref[:, j * COL_BLOCK:(j + 1) * COL_BLOCK] = kind[j](j, project(j)).astype(BF16)


def _inproj(x2, ng, wlo, whi, wab, lng, lnb, tabc, tabs, alog, dtb, conv_w, *, seq, layer):
    t = x2.shape[0]
    tm = INPROJ_ROWS
    ncols = N_COL_BLOCKS * COL_BLOCK
    const = lambda i: (0, 0)
    rows = lambda i: (i, 0)
    lyr = lambda i: (layer, 0, 0)
    resident = pl.Buffered(1)
    return pl.pallas_call(
        functools.partial(_inproj_kernel, tiles_per_seq=seq // tm),
        grid=(t // tm,),
        in_specs=[
            pl.BlockSpec((tm, D_MODEL), rows),
            pl.BlockSpec((1, D_MODEL), const),
            pl.BlockSpec((None, D_MODEL, wlo.shape[2]), lyr, pipeline_mode=resident),
            pl.BlockSpec((None, D_MODEL, whi.shape[2]), lyr, pipeline_mode=resident),
            pl.BlockSpec((None, D_MODEL, LANES), lyr, pipeline_mode=resident),
            pl.BlockSpec((1, COL_BLOCK), const),
            pl.BlockSpec((1, COL_BLOCK), const),
            pl.BlockSpec((tm, LANES), rows),
            pl.BlockSpec((tm, LANES), rows),
            pl.BlockSpec((1, LANES), const),
            pl.BlockSpec((1, LANES), const),
            pl.BlockSpec((DN_CONV, 3 * DN_WIDTH), const),
        ],
        out_specs=[
            pl.BlockSpec((tm, ncols), rows),
            pl.BlockSpec((tm, LANES), rows),
        ],
        out_shape=[
            jax.ShapeDtypeStruct((t, ncols), BF16),
            jax.ShapeDtypeStruct((t, LANES), F32),
        ],
        scratch_shapes=[pltpu.VMEM((tm, D_MODEL), BF16),
                        pltpu.VMEM((3, CONV_PAD, COL_BLOCK), F32)],
        compiler_params=pltpu.CompilerParams(
            dimension_semantics=("arbitrary",), vmem_limit_bytes=VMEM_LIMIT),
        name="inproj",
    )(x2, ng, wlo, whi, wab, lng, lnb, tabc, tabs, alog, dtb, conv_w)


def _attn_kernel(q_ref, k_ref, v_ref, z_ref, lam_ref, sg_ref, o_ref, vt_ref, *, lambda_init, tq):
    for hh in range(ATTN_HEADS_PER_STEP):
        lanes = slice(hh * LANES, (hh + 1) * LANES)
        _attn_head(q_ref.at[:, lanes], k_ref.at[:, lanes], v_ref.at[:, lanes], z_ref.at[:, lanes],
                   lam_ref, sg_ref, o_ref.at[:, lanes], vt_ref.at[hh], lambda_init=lambda_init, tq=tq)


def _attn_head(q_ref, k_ref, v_ref, z_ref, lam_ref, sg_ref, o_ref, vt_ref, *, lambda_init, tq):
    seq = q_ref.shape[0]
    tk = tq
    nq = seq // tq

    lp = lam_ref[...]
    lam = (jnp.exp(jnp.sum(lp[0:1] * lp[1:2], axis=-1, keepdims=True))
           - jnp.exp(jnp.sum(lp[2:3] * lp[3:4], axis=-1, keepdims=True)) + lambda_init)

    for c0 in range(0, seq, tk):
        vt_ref[:, c0:c0 + tk] = v_ref[c0:c0 + tk, :].astype(F32).T.astype(BF16)

    def scores(qs, kb):
        ks = k_ref[kb * tk:(kb + 1) * tk, :]
        return [_nt_dot(ks, qs[c]) for c in range(2)]

    def online_update(st, m, l, a, vts):
        mn = jnp.maximum(m, jnp.max(st, axis=0, keepdims=True))
        alpha = jnp.exp2(m - mn)
        pt = jnp.exp2(st - mn)
        l = alpha * l + jnp.sum(pt, axis=0, keepdims=True)
        a = alpha * a + jnp.dot(vts, pt.astype(BF16), preferred_element_type=F32)
        return mn, l, a

    def softmax_pv(kb, sts, carry):
        vts = vt_ref[:, kb * tk:(kb + 1) * tk]
        return tuple(online_update(sts[c], *carry[c], vts) for c in range(2))

    half = tq // 2

    def diag_scores(qs, qi):
        ks = k_ref[qi * tk:(qi + 1) * tk, :]
        return [(_nt_dot(ks[:half, :], qs[c][:half, :]),
                 _nt_dot(ks, qs[c][half:, :])) for c in range(2)]

    def diag_softmax_pv(qi, dsc, carry):
        vts = vt_ref[:, qi * tk:(qi + 1) * tk]
        ri = lax.broadcasted_iota(jnp.int32, (half, half), 0)
        ci = lax.broadcasted_iota(jnp.int32, (half, half), 1)
        tri = ri <= ci
        out = []
        for c in range(2):
            m, l, a = carry[c]
            st_l, st_r = dsc[c]
            st_l = jnp.where(tri, st_l, -jnp.inf)
            st_r = jnp.concatenate([st_r[:half, :], jnp.where(tri, st_r[half:, :], -jnp.inf)], axis=0)
            ml, ll, al = online_update(st_l, m[:, :half], l[:, :half], a[:, :half], vts[:, :half])
            mr, lr, ar = online_update(st_r, m[:, half:], l[:, half:], a[:, half:], vts)
            out.append((jnp.concatenate([ml, mr], axis=1), jnp.concatenate([ll, lr], axis=1),
                        jnp.concatenate([al, ar], axis=1)))
        return tuple(out)

    for qi in range(nq):
        qr = slice(qi * tq, (qi + 1) * tq)
        q = q_ref[qr, :]
        lane = lax.broadcasted_iota(jnp.int32, q.shape, 1)
        zero = jnp.zeros_like(q)
        qs = (jnp.where(lane < DA_HEAD_DIM, q, zero), jnp.where(lane >= DA_HEAD_DIM, q, zero))
        carry = tuple((jnp.full((1, tq), -1e30, F32), jnp.zeros((1, tq), F32),
                       jnp.zeros((LANES, tq), F32)) for _ in range(2))
        sts = scores(qs, 0) if qi > 0 else None
        dsc = diag_scores(qs, qi) if qi == 0 else None
        for kb in range(qi):
            if kb + 1 < qi:
                nxt = scores(qs, kb + 1)
            else:
                nxt, dsc = None, diag_scores(qs, qi)
            carry = softmax_pv(kb, sts, carry)
            sts = nxt
        (_, l0, a0), (_, l1, a1) = diag_softmax_pv(qi, dsc, carry)

        ot = a0 / l0 - lam * (a1 / l1)
        ot = ot * lax.rsqrt(jnp.mean(ot * ot, axis=0, keepdims=True) + EPS)
        o = ot.T * sg_ref[...] * (1.0 - lambda_init)
        o_ref[qr, :] = (o * z_ref[qr, :].astype(F32)).astype(BF16)


def _attn(proj, lam_p, subln_g, *, batch, seq, lambda_init):
    t = proj.shape[0]
    per = COL_BLOCK // LANES
    tq = min(seq, ATTN_BLOCK)
    hp = ATTN_HEADS_PER_STEP
    width = hp * LANES
    col = lambda cb: (lambda b, h: (b, (cb * per) // hp + h))
    return pl.pallas_call(
        functools.partial(_attn_kernel, lambda_init=lambda_init, tq=tq),
        grid=(batch, DA_HEADS // hp),
        in_specs=[
            pl.BlockSpec((seq, width), col(CB_DA_Q)),
            pl.BlockSpec((seq, width), col(CB_DA_K)),
            pl.BlockSpec((seq, width), col(CB_DA_V)),
            pl.BlockSpec((seq, width), col(CB_DA_Z)),
            pl.BlockSpec((8, LANES), lambda b, h: (0, 0)),
            pl.BlockSpec((1, LANES), lambda b, h: (0, 0)),
        ],
        out_specs=pl.BlockSpec((seq, width), lambda b, h: (b, h)),
        out_shape=jax.ShapeDtypeStruct((t, DA_HEADS * LANES), BF16),
        scratch_shapes=[pltpu.VMEM((hp, LANES, seq), BF16)],
        compiler_params=pltpu.CompilerParams(
            dimension_semantics=("arbitrary", "arbitrary"), vmem_limit_bytes=VMEM_LIMIT),
        name="attn",
    )(proj, proj, proj, proj, lam_p, subln_g)


def _segment_cumsum(x, axis, seg):
    idx = lax.broadcasted_iota(jnp.int32, x.shape, axis) & (seg - 1)
    s = 1
    while s < seg:
        x = x + jnp.where(idx >= s, pltpu.roll(x, s, axis), 0.0)
        s *= 2
    return x


def _gdn_kernel(qkv_ref, z_ref, ab_ref, ng_ref, o_ref, st_ref):
    seq = qkv_ref.shape[0]
    cr = DN_CHUNK_ROWS
    ngroups = seq // cr
    nh = DN_HEADS

    st_ref[...] = jnp.zeros(st_ref.shape, F32)

    def load(r, part, hh):
        col = part * DN_WIDTH + hh * LANES
        return qkv_ref[r, col:col + LANES]

    def group(g, carry):
        r = pl.ds(pl.multiple_of(g * cr, cr), cr)
        abv = ab_ref[r, :]
        gcol = _segment_cumsum(abv, 0, cr)
        grow = gcol.T
        ri = lax.broadcasted_iota(jnp.int32, (cr, cr), 0)
        ci = lax.broadcasted_iota(jnp.int32, (cr, cr), 1)
        causal = ri >= ci
        strict = ri > ci
        heads = range(nh)
        dot = functools.partial(jnp.dot, preferred_element_type=F32)
        gcc = [gcol[:, hh:hh + 1] for hh in heads]
        bcol = [abv[:, nh + hh:nh + hh + 1] for hh in heads]
        decay = [jnp.exp(jnp.where(causal, gcc[hh] - grow[hh:hh + 1, :], -jnp.inf)) for hh in heads]
        q16 = [load(r, 0, hh) for hh in heads]
        k16 = [load(r, 1, hh) for hh in heads]
        qf = [q16[hh].astype(F32) for hh in heads]
        kf = [k16[hh].astype(F32) for hh in heads]
        vf = [load(r, 2, hh).astype(F32) for hh in heads]
        a = [jnp.where(strict, (bcol[hh] * _nt_dot(k16[hh], k16[hh])) * decay[hh], 0.0) for hh in heads]
        n = [-a[hh] for hh in heads]
        pw16 = [a[hh].astype(BF16) for hh in heads]
        for level in range(1, int(math.log2(cr))):
            lo = 2 ** level if 2 ** level >= BF16_SUBLANES else 0
            pw = [dot(pw16[hh][lo:, :], pw16[hh]) for hh in heads]
            pw16 = [pw[hh].astype(BF16) for hh in heads]
            if lo:
                pw16 = [jnp.concatenate([jnp.zeros((lo, cr), BF16), pw16[hh]], axis=0) for hh in heads]
            nlo = [n[hh][lo:, :] for hh in heads]
            nlo = [nlo[hh] + pw[hh] + dot(nlo[hh].astype(BF16), pw16[hh]) for hh in heads]
            n = [jnp.concatenate([n[hh][:lo, :], nlo[hh]], axis=0) if lo else nlo[hh] for hh in heads]
        eg = [jnp.exp(gcc[hh]) for hh in heads]
        rhs = [jnp.concatenate([vf[hh] * bcol[hh], kf[hh] * (bcol[hh] * eg[hh])], axis=1) for hh in heads]
        uw = [rhs[hh] + dot(n[hh].astype(BF16), rhs[hh].astype(BF16)) for hh in heads]
        qk = [_nt_dot(q16[hh], k16[hh]) * decay[hh] for hh in heads]
        gl = [gcc[hh][cr - 1:cr, :] for hh in heads]
        kdt = [(kf[hh] * jnp.exp(gl[hh] - gcc[hh])).T.astype(BF16) for hh in heads]
        st = [st_ref[hh] for hh in heads]
        st16 = [st[hh].astype(BF16) for hh in heads]
        v16 = [(uw[hh][:, :LANES] - dot(uw[hh][:, LANES:].astype(BF16), st16[hh])).astype(BF16) for hh in heads]
        o = [dot((qf[hh] * eg[hh]).astype(BF16), st16[hh]) + dot(qk[hh].astype(BF16), v16[hh]) for hh in heads]
        for hh in heads:
            st_ref[hh] = st[hh] * jnp.exp(gl[hh]) + dot(kdt[hh], v16[hh])
            oh = o[hh] * lax.rsqrt(jnp.mean(o[hh] * o[hh], axis=-1, keepdims=True) + EPS) * ng_ref[...]
            zc = z_ref[r, hh * LANES:(hh + 1) * LANES].astype(F32)
            o_ref[r, hh * LANES:(hh + 1) * LANES] = (oh * zc).astype(BF16)
        return carry

    lax.fori_loop(0, ngroups, group, 0, unroll=DN_GROUP_UNROLL)


def _gdn(proj, ab, norm_g, *, batch, seq):
    t = proj.shape[0]
    return pl.pallas_call(
        _gdn_kernel,
        grid=(batch,),
        in_specs=[
            pl.BlockSpec((seq, 3 * DN_WIDTH), lambda b: (b, CB_DN_QKV // 3)),
            pl.BlockSpec((seq, DN_WIDTH), lambda b: (b, CB_DN_Z)),
            pl.BlockSpec((seq, LANES), lambda b: (b, 0)),
            pl.BlockSpec((1, LANES), lambda b: (0, 0)),
        ],
        out_specs=pl.BlockSpec((seq, DN_WIDTH), lambda b: (b, 0)),
        out_shape=jax.ShapeDtypeStruct((t, DN_WIDTH), BF16),
        scratch_shapes=[pltpu.VMEM((DN_HEADS, DN_HEAD_DIM, DN_HEAD_DIM), F32)],
        compiler_params=pltpu.CompilerParams(
            dimension_semantics=("arbitrary",), vmem_limit_bytes=VMEM_LIMIT),
        name="gdn",
    )(proj, proj, ab, norm_g)


def _merge_kernel(x_ref, gates_ref, gm_ref, yb_ref, yc_ref, p_ref, ws_ref, bst_ref,
                  wa_ref, wb_ref, wc_ref, wo_ref, png_ref, wpg_ref, wpp_ref, fng_ref,
                  o_ref, ya_ref, *, rc, final):
    tm = x_ref.shape[0]

    ri = lax.broadcasted_iota(jnp.int32, (GM_CHUNK, GM_CHUNK), 0)
    ci = lax.broadcasted_iota(jnp.int32, (GM_CHUNK, GM_CHUNK), 1)
    wtri = [jnp.where(ri >= ci, ws_ref[g], 0.0).astype(BF16) for g in range(GM_GROUPS)]

    for c in range(tm // GM_CHUNK):
        r = slice(c * GM_CHUNK, (c + 1) * GM_CHUNK)
        for g in range(GM_GROUPS):
            cols = slice(g * LANES, (g + 1) * LANES)
            u = gm_ref[r, cols].astype(F32)
            v = gm_ref[r, GM_WIDTH + g * LANES:GM_WIDTH + (g + 1) * LANES]
            z = gm_ref[r, 2 * GM_WIDTH + g * LANES:2 * GM_WIDTH + (g + 1) * LANES].astype(F32)
            mix = jnp.dot(wtri[g], v, preferred_element_type=F32) + bst_ref[:, g:g + 1]
            ya_ref[r, cols] = (u * mix * z).astype(BF16)

    for c in range(tm // rc):
        r = slice(c * rc, (c + 1) * rc)
        ga = gates_ref[r, 0:D_MODEL].astype(F32)
        gb = gates_ref[r, D_MODEL:2 * D_MODEL].astype(F32)
        gc = gates_ref[r, 2 * D_MODEL:3 * D_MODEL].astype(F32)
        m = (ga * jnp.dot(ya_ref[r, :], wa_ref[...], preferred_element_type=F32)
             + gb * jnp.dot(yb_ref[r, :], wb_ref[...], preferred_element_type=F32)
             + gc * jnp.dot(yc_ref[r, :], wc_ref[...], preferred_element_type=F32))
        x1 = x_ref[r, :] + jnp.dot(m.astype(BF16), wo_ref[...], preferred_element_type=F32)
        hn = x1 * lax.rsqrt(jnp.mean(x1 * x1, axis=-1, keepdims=True) + EPS) * png_ref[...]
        gate = _sigmoid(jnp.dot(hn.astype(BF16), wpg_ref[...], preferred_element_type=F32))
        pp = jnp.dot(p_ref[r, :].astype(BF16), wpp_ref[...], preferred_element_type=F32)
        x2 = x1 + gate * pp
        if final:
            x2 = x2 * lax.rsqrt(jnp.mean(x2 * x2, axis=-1, keepdims=True) + EPS) * fng_ref[...]
        o_ref[r, :] = x2


def _merge(x2, proj, yb, yc, p3, ws, bst, wa, wb, wc, wo, png, wpg, wpp, fng, *, final, layer):
    t = x2.shape[0]
    tm = min(t, MERGE_ROWS)
    rc = MERGE_CHUNK
    rows = lambda i: (i, 0)
    const2 = lambda i: (0, 0)
    lyr = lambda i: (layer, 0, 0)
    resident = pl.Buffered(1)
    wspec = lambda k: pl.BlockSpec((None, k, D_MODEL), lyr, pipeline_mode=resident)
    return pl.pallas_call(
        functools.partial(_merge_kernel, rc=rc, final=final),
        grid=(t // tm,),
        in_specs=[
            pl.BlockSpec((tm, D_MODEL), rows),
            pl.BlockSpec((tm, 3 * D_MODEL), lambda i: (i, CB_GATES)),
            pl.BlockSpec((tm, 3 * GM_WIDTH), lambda i: (i, CB_GM_U // 3)),
            pl.BlockSpec((tm, DA_HEADS * LANES), rows),
            pl.BlockSpec((tm, DN_WIDTH), rows),
            pl.BlockSpec((None, tm, PLE_DIM), lambda i: (layer, i, 0)),
            pl.BlockSpec((None, GM_GROUPS, GM_CHUNK, GM_CHUNK), lambda i: (layer, 0, 0, 0)),
            pl.BlockSpec((GM_CHUNK, GM_GROUPS), const2),
            wspec(GM_WIDTH),
            wspec(DA_HEADS * LANES),
            wspec(DN_WIDTH),
            wspec(D_MODEL),
            pl.BlockSpec((1, D_MODEL), const2),
            wspec(D_MODEL),
            wspec(PLE_DIM),
            pl.BlockSpec((1, D_MODEL), const2),
        ],
        out_specs=pl.BlockSpec((tm, D_MODEL), rows),
        out_shape=jax.ShapeDtypeStruct((t, D_MODEL), F32),
        scratch_shapes=[pltpu.VMEM((tm, GM_WIDTH), BF16)],
        compiler_params=pltpu.CompilerParams(
            dimension_semantics=("arbitrary",), vmem_limit_bytes=VMEM_LIMIT),
        name="merge",
    )(x2, proj, proj, yb, yc, p3, ws, bst, wa, wb, wc, wo, png, wpg, wpp, fng)


def _split_w_in(w):
    w16 = w.astype(BF16)
    lo = w16[..., :W_IN_AB_START]
    hi = w16[..., W_IN_AB_START + 2 * DN_HEADS:]
    ab = jnp.pad(w16[..., W_IN_AB_START:W_IN_AB_START + 2 * DN_HEADS],
                 ((0, 0), (0, 0), (0, LANES - 2 * DN_HEADS)))
    return lo, hi, ab


def _rotary_tables(positions):
    inv_freq = ROPE_THETA ** (-jnp.arange(0, ROPE_DIM, 2, dtype=F32) / ROPE_DIM)
    ang = positions.astype(F32)[..., None] * inv_freq
    cos, sin = jnp.cos(ang), jnp.sin(ang)
    rest = DA_HEAD_DIM - ROPE_DIM
    tc = jnp.concatenate([cos, cos, jnp.ones(ang.shape[:-1] + (rest,), F32)], axis=-1)
    ts = jnp.concatenate([-sin, sin, jnp.zeros(ang.shape[:-1] + (rest,), F32)], axis=-1)
    rep = LANES // DA_HEAD_DIM
    tc = jnp.tile(tc, (1, 1, rep)).reshape(-1, LANES)
    ts = jnp.tile(ts, (1, 1, rep)).reshape(-1, LANES)
    return tc, ts


def _row(v, width=None):
    v = v.reshape(1, -1).astype(F32)
    if width is not None and v.shape[1] < width:
        v = jnp.pad(v, ((0, 0), (0, width - v.shape[1])))
    return v


def kernel(x, p, positions, norm_g, w_in, gm_ln_g, gm_ln_b, gm_ws, gm_bs, da_lq1, da_lk1, da_lq2, da_lk2, da_subln_g, dn_conv_w, dn_a_log, dn_dt_bias, dn_norm_g, w_br_a, w_br_b, w_br_c, w_out, ple_norm_g, w_ple_gate, w_ple_proj, final_norm_g):
    batch, seq, _ = x.shape
    depth = w_in.shape[0]
    t = batch * seq
    assert seq % DN_CHUNK_ROWS == 0 and seq % INPROJ_ROWS == 0

    w_lo, w_hi, w_ab = _split_w_in(w_in)
    tabc, tabs = _rotary_tables(positions)
    wa16, wb16, wc16 = w_br_a.astype(BF16), w_br_b.astype(BF16), w_br_c.astype(BF16)
    wo16, wpg16, wpp16 = w_out.astype(BF16), w_ple_gate.astype(BF16), w_ple_proj.astype(BF16)
    fng = _row(final_norm_g)
    p3 = p.reshape(depth, t, PLE_DIM)
    gm_ws = gm_ws.astype(F32)

    xc = x.reshape(t, D_MODEL)
    for i in range(depth):
        lambda_init = 0.8 - 0.6 * math.exp(-0.3 * i)
        proj, ab = _inproj(
            xc, _row(norm_g[i]), w_lo, w_hi, w_ab, _row(gm_ln_g[i]), _row(gm_ln_b[i]), tabc, tabs,
            _row(dn_a_log[i], LANES), _row(dn_dt_bias[i], LANES), dn_conv_w[i].astype(F32),
            seq=seq, layer=i)
        lam_p = jnp.pad(jnp.stack([da_lq1[i], da_lk1[i], da_lq2[i], da_lk2[i]]).astype(F32),
                        ((0, 4), (0, LANES - DA_HEAD_DIM)))
        yb = _attn(proj, lam_p, _row(da_subln_g[i]), batch=batch, seq=seq, lambda_init=lambda_init)
        yc = _gdn(proj, ab, _row(dn_norm_g[i]), batch=batch, seq=seq)
        xc = _merge(xc, proj, yb, yc, p3, gm_ws, gm_bs[i].T.astype(F32), wa16, wb16, wc16, wo16,
                    _row(ple_norm_g[i]), wpg16, wpp16, fng, final=(i == depth - 1), layer=i)
    return xc.reshape(batch, seq, D_MODEL)
```
